```python
import math
import jax, jax.numpy as jnp
from jax import lax
import numpy as np

D_MODEL = 1024
BATCH = 16
SEQ = 2048
DEPTH = 4

CTX_LEN = 256
GRID_W = 64
N_EVEN = (DEPTH + 1) // 2
N_ODD = DEPTH // 2
N_MOD = 9
D_FF = 2816
FFN_RES = 0.5
EPS = 1e-6
Q_BLOCK = 128
ROPE_THETA = 10000.0

CONV_CH = 512
CONV_K = 31
MLA_HEADS = 8
MLA_NOPE = 64
MLA_ROPE = 32
MLA_V = 64
Q_LORA = 384
KV_LORA = 256
MLA_SCALE = (MLA_NOPE + MLA_ROPE) ** -0.5
GQA_HEADS = 8
GQA_KV_HEADS = 2
GQA_GROUP = GQA_HEADS // GQA_KV_HEADS
GQA_HD = 64
GQA_SCALE = GQA_HD ** -0.5
FNET_GROUPS = 4
FNET_CH = 128

EVEN_IN = 2 * CONV_CH + Q_LORA + KV_LORA + MLA_ROPE
EVEN_MIX = CONV_CH + MLA_HEADS * MLA_V
ODD_Q = GQA_HEADS * GQA_HD
ODD_KV = GQA_KV_HEADS * GQA_HD
ODD_F = FNET_GROUPS * FNET_CH
ODD_IN = ODD_Q + 2 * ODD_KV + ODD_F
ODD_MIX = ODD_Q + ODD_F

kernel_name = 'hybrid_conv_mla_gqa_fnet_diffusion_trunk'


def rmsnorm(x, g):
    xf = x.astype(jnp.float32)
    y = xf * lax.rsqrt(jnp.mean(xf * xf, axis=-1, keepdims=True) + EPS)
    return (y * g.astype(jnp.float32)).astype(x.dtype)


def layernorm(x, g, b):
    xf = x.astype(jnp.float32)
    mu = jnp.mean(xf, axis=-1, keepdims=True)
    var = jnp.mean(jnp.square(xf - mu), axis=-1, keepdims=True)
    y = (xf - mu) * lax.rsqrt(var + EPS)
    return (y * g.astype(jnp.float32) + b.astype(jnp.float32)).astype(x.dtype)


def adaln(x, g, shift, scale):
    return rmsnorm(x, g) * (1.0 + scale) + shift


def swiglu(h, w_gate, w_up, w_down):
    return (jax.nn.silu(h @ w_gate) * (h @ w_up)) @ w_down


def ffn_sublayer(x, mod, k, g_pre, g_post, w_gate, w_up, w_down):
    h = adaln(x, g_pre, mod[:, :, 3 * k], mod[:, :, 3 * k + 1])
    y = swiglu(h, w_gate, w_up, w_down)
    return x + FFN_RES * mod[:, :, 3 * k + 2] * rmsnorm(y, g_post)


def rope_tables(row, col, rot_dim, dtype):
    nf = rot_dim // 4
    inv = ROPE_THETA ** (-jnp.arange(nf, dtype=jnp.float32) / nf)
    ang = jnp.concatenate([row[:, None].astype(jnp.float32) * inv,
                           col[:, None].astype(jnp.float32) * inv], axis=-1)
    return jnp.cos(ang).astype(dtype), jnp.sin(ang).astype(dtype)


def rope_2d(x, cos, sin):
    nf = x.shape[-1] // 4
    xp = x.reshape(x.shape[:-1] + (2, 2, nf))
    x1, x2 = xp[..., 0, :], xp[..., 1, :]
    c = cos.reshape(cos.shape[0], 1, 2, nf)
    s = sin.reshape(sin.shape[0], 1, 2, nf)
    return jnp.stack([x1 * c - x2 * s, x1 * s + x2 * c], axis=-2).reshape(x.shape)


def block_attention(q, k, v, scale):
    B, Lq, Hk, G, dk = q.shape
    nblk = Lq // Q_BLOCK
    qb = jnp.moveaxis(q.reshape(B, nblk, Q_BLOCK, Hk, G, dk), 1, 0)

    def one_block(qi):
        s = jnp.einsum('bqhgd,bkhd->bhgqk', qi, k, preferred_element_type=jnp.float32) * scale
        p = jax.nn.softmax(s, axis=-1).astype(v.dtype)
        return jnp.einsum('bhgqk,bkhe->bqhge', p, v)

    o = lax.map(one_block, qb)
    return jnp.moveaxis(o, 0, 1).reshape(B, Lq, Hk, G, v.shape[-1])


def depthwise_conv(u, w, b):
    y = lax.conv_general_dilated(u, w[:, None, :].astype(u.dtype), window_strides=(1,),
                                 padding=[(CONV_K // 2, CONV_K // 2)],
                                 dimension_numbers=('NWC', 'WIO', 'NWC'),
                                 feature_group_count=u.shape[-1])
    return y + b


def conformer_conv(val, gate, conv_w, conv_b, ln_g, ln_b):
    u = val * jax.nn.sigmoid(gate)
    u = depthwise_conv(u, conv_w, conv_b)
    u = layernorm(u, ln_g, ln_b)
    return jax.nn.silu(u)


def fourier_mix(f):
    B, L, _ = f.shape
    fg = f.reshape(B, L, FNET_GROUPS, FNET_CH).astype(jnp.float32)
    out = jnp.real(jnp.fft.fft2(fg, axes=(1, 3), norm='ortho'))
    return out.reshape(B, L, ODD_F).astype(f.dtype)


def mla_queries(cq, g_cq, w_uq, rot):
    B, L, _ = cq.shape
    q = (rmsnorm(cq, g_cq) @ w_uq).reshape(B, L, MLA_HEADS, MLA_NOPE + MLA_ROPE)
    if rot is not None:
        q = jnp.concatenate([q[..., :MLA_NOPE], rope_2d(q[..., MLA_NOPE:], *rot)], axis=-1)
    return q[:, :, :, None, :]


def mla_keys_values(ckv, kpe, g_ckv, w_ukv, rot):
    B, L, _ = ckv.shape
    kv = (rmsnorm(ckv, g_ckv) @ w_ukv).reshape(B, L, MLA_HEADS, MLA_NOPE + MLA_V)
    kpe = kpe[:, :, None, :]
    if rot is not None:
        kpe = rope_2d(kpe, *rot)
    k = jnp.concatenate([kv[..., :MLA_NOPE],
                         jnp.broadcast_to(kpe, (B, L, MLA_HEADS, MLA_ROPE))], axis=-1)
    return k, kv[..., MLA_NOPE:]


def even_mixer(h, hc, w_in, conv_w, conv_b, ln_g, ln_b, g_cq, w_uq, g_ckv, w_ukv, w_out,
               rot, need_ctx):
    B, L, _ = h.shape
    Lc = hc.shape[1]
    a_cols = 2 * CONV_CH
    splits = [CONV_CH, a_cols, a_cols + Q_LORA, a_cols + Q_LORA + KV_LORA]
    a_val, a_gate, cq, ckv, kpe = jnp.split(h @ w_in, splits, axis=-1)
    cq_c, ckv_c, kpe_c = jnp.split(hc @ w_in[:, a_cols:], [Q_LORA, Q_LORA + KV_LORA], axis=-1)
    k_c, v_c = mla_keys_values(ckv_c, kpe_c, g_ckv, w_ukv, None)
    k, v = mla_keys_values(ckv, kpe, g_ckv, w_ukv, rot)
    q = mla_queries(cq, g_cq, w_uq, rot)
    o_b = block_attention(q, jnp.concatenate([k_c, k], axis=1), jnp.concatenate([v_c, v], axis=1),
                          MLA_SCALE).reshape(B, L, MLA_HEADS * MLA_V)
    o_a = conformer_conv(a_val, a_gate, conv_w, conv_b, ln_g, ln_b)
    y = jnp.concatenate([o_a, o_b], axis=-1) @ w_out
    if not need_ctx:
        return y, None
    a_val_c, a_gate_c = jnp.split(hc @ w_in[:, :a_cols], 2, axis=-1)
    q_c = mla_queries(cq_c, g_cq, w_uq, None)
    o_bc = block_attention(q_c, k_c, v_c, MLA_SCALE).reshape(B, Lc, MLA_HEADS * MLA_V)
    o_ac = conformer_conv(a_val_c, a_gate_c, conv_w, conv_b, ln_g, ln_b)
    y_c = jnp.concatenate([o_ac, o_bc], axis=-1) @ w_out
    return y, y_c


def odd_mixer(h, hc, w_in, g_qn, g_kn, w_out, rot, need_ctx):
    B, L, _ = h.shape
    Lc = hc.shape[1]
    kv_lo, kv_hi = ODD_Q, ODD_Q + 2 * ODD_KV
    q, k, v, f = jnp.split(h @ w_in, [ODD_Q, ODD_Q + ODD_KV, kv_hi], axis=-1)
    q = rope_2d(rmsnorm(q.reshape(B, L, GQA_HEADS, GQA_HD), g_qn), *rot)
    k = rope_2d(rmsnorm(k.reshape(B, L, GQA_KV_HEADS, GQA_HD), g_kn), *rot)
    v = v.reshape(B, L, GQA_KV_HEADS, GQA_HD)
    k_c, v_c = jnp.split(hc @ w_in[:, kv_lo:kv_hi], 2, axis=-1)
    k_c = rmsnorm(k_c.reshape(B, Lc, GQA_KV_HEADS, GQA_HD), g_kn)
    v_c = v_c.reshape(B, Lc, GQA_KV_HEADS, GQA_HD)
    o_c = block_attention(q.reshape(B, L, GQA_KV_HEADS, GQA_GROUP, GQA_HD),
                          jnp.concatenate([k_c, k], axis=1), jnp.concatenate([v_c, v], axis=1),
                          GQA_SCALE).reshape(B, L, ODD_Q)
    o_d = fourier_mix(f)
    y = jnp.concatenate([o_c, o_d], axis=-1) @ w_out
    if not need_ctx:
        return y, None
    q_c = rmsnorm((hc @ w_in[:, :ODD_Q]).reshape(B, Lc, GQA_HEADS, GQA_HD), g_qn)
    o_cc = block_attention(q_c.reshape(B, Lc, GQA_KV_HEADS, GQA_GROUP, GQA_HD), k_c, v_c,
                           GQA_SCALE).reshape(B, Lc, ODD_Q)
    o_dc = fourier_mix(hc @ w_in[:, kv_hi:])
    y_c = jnp.concatenate([o_cc, o_dc], axis=-1) @ w_out
    return y, y_c


def setup_inputs(seed: int = 0) -> dict:
    key = jax.random.key(seed)
    keys = jax.random.split(key, 32)
    ks = iter([keys[i] for i in range(32)])
    D = D_MODEL

    def nrm(shape, scale):
        return jax.random.normal(next(ks), shape, jnp.float32) * scale

    def gain(shape):
        return 1.0 + nrm(shape, 0.02)

    return {
        'x': nrm((BATCH, SEQ, D), 1.0),
        'c': nrm((BATCH, D), 1.0),
        'ctx': nrm((BATCH, CTX_LEN, D), 1.0),
        'c_ctx': nrm((D,), 1.0),
        'w_mod': nrm((DEPTH, D, N_MOD * D), 0.5 * D ** -0.5),
        'b_mod': nrm((DEPTH, N_MOD * D), 0.02),
        'g_pre': gain((DEPTH, 3, D)),
        'g_post': gain((DEPTH, 3, D)),
        'ffn_w_gate': nrm((DEPTH, 2, D, D_FF), D ** -0.5),
        'ffn_w_up': nrm((DEPTH, 2, D, D_FF), D ** -0.5),
        'ffn_w_down': nrm((DEPTH, 2, D_FF, D), D_FF ** -0.5),
        'ev_w_in': nrm((N_EVEN, D, EVEN_IN), D ** -0.5),
        'ev_conv_w': nrm((N_EVEN, CONV_K, CONV_CH), CONV_K ** -0.5),
        'ev_conv_b': nrm((N_EVEN, CONV_CH), 0.02),
        'ev_ln_g': gain((N_EVEN, CONV_CH)),
        'ev_ln_b': nrm((N_EVEN, CONV_CH), 0.02),
        'ev_g_cq': gain((N_EVEN, Q_LORA)),
        'ev_w_uq': nrm((N_EVEN, Q_LORA, MLA_HEADS * (MLA_NOPE + MLA_ROPE)), Q_LORA ** -0.5),
        'ev_g_ckv': gain((N_EVEN, KV_LORA)),
        'ev_w_ukv': nrm((N_EVEN, KV_LORA, MLA_HEADS * (MLA_NOPE + MLA_V)), KV_LORA ** -0.5),
        'ev_w_out': nrm((N_EVEN, EVEN_MIX, D), EVEN_MIX ** -0.5),
        'od_w_in': nrm((N_ODD, D, ODD_IN), D ** -0.5),
        'od_g_qn': gain((N_ODD, GQA_HD)),
        'od_g_kn': gain((N_ODD, GQA_HD)),
        'od_w_out': nrm((N_ODD, ODD_MIX, D), ODD_MIX ** -0.5),
    }


def reference(x, c, ctx, c_ctx, w_mod, b_mod, g_pre, g_post, ffn_w_gate, ffn_w_up, ffn_w_down,
              ev_w_in, ev_conv_w, ev_conv_b, ev_ln_g, ev_ln_b, ev_g_cq, ev_w_uq, ev_g_ckv,
              ev_w_ukv, ev_w_out, od_w_in, od_g_qn, od_g_kn, od_w_out):
    B, L, D = x.shape
    ROWS = L // GRID_W
    row = jnp.repeat(jnp.arange(ROWS), GRID_W)
    col = jnp.tile(jnp.arange(GRID_W), ROWS)
    rot_mla = rope_tables(row, col, MLA_ROPE, x.dtype)
    rot_gqa = rope_tables(row, col, GQA_HD, x.dtype)
    sc = jax.nn.silu(c)
    scc = jax.nn.silu(c_ctx)
    xc = ctx
    for i in range(DEPTH):
        need_ctx = i < DEPTH - 1
        mod = (sc @ w_mod[i] + b_mod[i]).reshape(B, 1, N_MOD, D)
        modc = (scc @ w_mod[i] + b_mod[i]).reshape(1, 1, N_MOD, D)
        x = ffn_sublayer(x, mod, 0, g_pre[i, 0], g_post[i, 0],
                         ffn_w_gate[i, 0], ffn_w_up[i, 0], ffn_w_down[i, 0])
        xc = ffn_sublayer(xc, modc, 0, g_pre[i, 0], g_post[i, 0],
                          ffn_w_gate[i, 0], ffn_w_up[i, 0], ffn_w_down[i, 0])
        h = adaln(x, g_pre[i, 1], mod[:, :, 3], mod[:, :, 4])
        hc = adaln(xc, g_pre[i, 1], modc[:, :, 3], modc[:, :, 4])
        j = i // 2
        if i % 2 == 0:
            y, y_c = even_mixer(h, hc, ev_w_in[j], ev_conv_w[j], ev_conv_b[j], ev_ln_g[j], ev_ln_b[j],
                                ev_g_cq[j], ev_w_uq[j], ev_g_ckv[j], ev_w_ukv[j], ev_w_out[j],
                                rot_mla, need_ctx)
        else:
            y, y_c = odd_mixer(h, hc, od_w_in[j], od_g_qn[j], od_g_kn[j], od_w_out[j],
                               rot_gqa, need_ctx)
        x = x + mod[:, :, 5] * rmsnorm(y, g_post[i, 1])
        x = ffn_sublayer(x, mod, 2, g_pre[i, 2], g_post[i, 2],
                         ffn_w_gate[i, 1], ffn_w_up[i, 1], ffn_w_down[i, 1])
        if need_ctx:
            xc = xc + modc[:, :, 5] * rmsnorm(y_c, g_post[i, 1])
            xc = ffn_sublayer(xc, modc, 2, g_pre[i, 2], g_post[i, 2],
                              ffn_w_gate[i, 1], ffn_w_up[i, 1], ffn_w_down[i, 1])
    return x
```

```python
import functools
import math

import numpy as np
import jax
import jax.numpy as jnp
from jax import lax
from jax.experimental import pallas as pl
from jax.experimental.pallas import tpu as pltpu

F32 = jnp.float32
BF16 = jnp.bfloat16

N_MOD = 9
FFN_RES = 0.5
EPS = 1e-6
ROPE_THETA = 10000.0
GRID_W = 64
CONV_CH = 512
CONV_K = 31
MLA_HEADS = 8
MLA_NOPE = 64
MLA_ROPE = 32
MLA_V = 64
Q_LORA = 384
KV_LORA = 256
MLA_SCALE = (MLA_NOPE + MLA_ROPE) ** -0.5
GQA_HEADS = 8
GQA_KV_HEADS = 2
GQA_GROUP = GQA_HEADS // GQA_KV_HEADS
GQA_HD = 64
GQA_SCALE = GQA_HD ** -0.5
FNET_GROUPS = 4
FNET_CH = 128
HEADS = 8
HEAD_V = 64

LANE = 128
SUBLANE = 8
MXU_DIM = 256
VMEM_LIMIT = 56 * 1024 * 1024
FF_CHUNK = 256
CONV_HALO = 16
CONV_ROWS = 64


def _cparams(n_axes):
    return pltpu.CompilerParams(dimension_semantics=("arbitrary",) * n_axes,
                                vmem_limit_bytes=VMEM_LIMIT)


def _resident(shape):
    zeros = (0,) * len(shape)
    return pl.BlockSpec(shape, lambda *_: zeros, pipeline_mode=pl.Buffered(1))


def _rms(x):
    return x * lax.rsqrt(jnp.mean(x * x, axis=-1, keepdims=True) + EPS)


def _sigmoid(x):
    return 1.0 / (1.0 + jnp.exp(-x))


def _dot(a, b):
    return jnp.dot(a, b, preferred_element_type=F32)


def _dot_t(a, b):
    return lax.dot_general(a, b, (((1,), (1,)), ((), ())), preferred_element_type=F32)


def _mod_kernel(c_ref, w_ref, b_ref, o_ref):
    c = c_ref[...]
    sc = (c * _sigmoid(c)).astype(BF16)
    o_ref[...] = _dot(sc, w_ref[...].astype(BF16)) + b_ref[...]


def _modulation(c_all, w_mod, b_mod):
    depth, d, nd = w_mod.shape
    bp = c_all.shape[0]
    tn = 1024
    return pl.pallas_call(
        _mod_kernel,
        grid=(depth, nd // tn),
        in_specs=[pl.BlockSpec((bp, d), lambda i, j: (0, 0)),
                  pl.BlockSpec((None, d, tn), lambda i, j: (i, 0, j)),
                  pl.BlockSpec((None, 1, tn), lambda i, j: (i, 0, j))],
        out_specs=pl.BlockSpec((None, bp, tn), lambda i, j: (i, 0, j)),
        out_shape=jax.ShapeDtypeStruct((depth, bp, nd), F32),
        compiler_params=_cparams(2),
        name="modulation",
    )(c_all, w_mod, b_mod.reshape(depth, 1, nd))


def _ffn_body(x, mod_ref, k, gpre_ref, gpost_ref, wgu_ref, wd_ref, h_ref, u_ref):
    d = x.shape[-1]
    shift = mod_ref[:, (3 * k) * d:(3 * k + 1) * d]
    scale = mod_ref[:, (3 * k + 1) * d:(3 * k + 2) * d]
    gate = mod_ref[:, (3 * k + 2) * d:(3 * k + 3) * d]
    h = _rms(x) * gpre_ref[...] * (1.0 + scale) + shift
    h_ref[...] = h.astype(BF16)
    n_chunks, _, two_ch = wgu_ref.shape
    ch = two_ch // 2
    for c in range(n_chunks):
        ab = _dot(h_ref[...], wgu_ref[c])
        a = ab[:, :ch]
        b = ab[:, ch:]
        u_ref[:, c * ch:(c + 1) * ch] = (a * _sigmoid(a) * b).astype(BF16)
    y = _dot(u_ref[...], wd_ref[...])
    return x + (FFN_RES * gate) * (_rms(y) * gpost_ref[...])


def _ffn_kernel(x_ref, mod_ref, gpre_ref, gpost_ref, wgu_ref, wd_ref, o_ref, h_ref, u_ref, *, k):
    o_ref[...] = _ffn_body(x_ref[...], mod_ref, k, gpre_ref, gpost_ref, wgu_ref, wd_ref,
                           h_ref, u_ref)


def _mix_ffn_kernel(x_ref, oa_ref, ob_ref, wo_ref, gmix_ref, mod_ref, gpre_ref, gpost_ref,
                    wgu_ref, wd_ref, o_ref, h_ref, u_ref):
    x = x_ref[...]
    d = x.shape[-1]
    half = oa_ref.shape[-1]
    y = _dot(oa_ref[...], wo_ref[:half, :]) + _dot(ob_ref[...], wo_ref[half:, :])
    gate = mod_ref[:, 5 * d:6 * d]
    x1 = x + gate * (_rms(y) * gmix_ref[...])
    o_ref[...] = _ffn_body(x1, mod_ref, 2, gpre_ref, gpost_ref, wgu_ref, wd_ref, h_ref, u_ref)


def _row_tile(rows_per_batch, total_rows, want=512):
    tm = min(want, rows_per_batch)
    assert rows_per_batch % tm == 0 and total_rows % tm == 0
    return tm


def _mod_spec(nd, rows_per_batch, tm, fixed_row):
    if fixed_row is not None:
        return pl.BlockSpec((None, 1, nd), lambda t: (fixed_row, 0, 0))
    per = rows_per_batch // tm
    return pl.BlockSpec((None, 1, nd), lambda t: (t // per, 0, 0))


def _ffn(x, mod, k, g_pre, g_post, wgu, wd, rows_per_batch, fixed_row=None, mix=None):
    r, d = x.shape
    nd = mod.shape[-1]
    dff = wd.shape[0]
    tm = _row_tile(rows_per_batch, r)
    row = lambda t: (t, 0)
    vec = pl.BlockSpec((1, d), lambda t: (0, 0))
    tail_specs = [_mod_spec(nd, rows_per_batch, tm, fixed_row), vec, vec,
                  _resident(wgu.shape), _resident(wd.shape)]
    tail_args = [mod, g_pre.reshape(1, d), g_post.reshape(1, d), wgu, wd]
    scratch = [pltpu.VMEM((tm, d), BF16), pltpu.VMEM((tm, dff), BF16)]
    if mix is None:
        kern = functools.partial(_ffn_kernel, k=k)
        in_specs = [pl.BlockSpec((tm, d), row)] + tail_specs
        args = [x] + tail_args
        name = "ffn"
    else:
        oa, ob, wo, g_mix = mix
        half = oa.shape[-1]
        kern = _mix_ffn_kernel
        in_specs = [pl.BlockSpec((tm, d), row), pl.BlockSpec((tm, half), row),
                    pl.BlockSpec((tm, half), row), _resident(wo.shape), vec] + tail_specs
        args = [x, oa, ob, wo, g_mix.reshape(1, d)] + tail_args
        name = "mix_ffn"
    return pl.pallas_call(
        kern,
        grid=(r // tm,),
        in_specs=in_specs,
        out_specs=pl.BlockSpec((tm, d), row),
        out_shape=jax.ShapeDtypeStruct((r, d), F32),
        scratch_shapes=scratch,
        compiler_params=_cparams(1),
        name=name,
    )(*args)


def _even_in_kernel(x_ref, mod_ref, gpre_ref, win_ref, gcq_ref, wq_ref, gckv_ref, wkv_ref, tab_ref,
                    u_ref, q_ref, k_ref, v_ref):
    x = x_ref[...]
    d = x.shape[-1]
    shift = mod_ref[:, 3 * d:4 * d]
    scale = mod_ref[:, 4 * d:5 * d]
    h = (_rms(x) * gpre_ref[...] * (1.0 + scale) + shift).astype(BF16)
    p = _dot(h, win_ref[...])
    c0 = CONV_CH
    c1 = 2 * CONV_CH
    c2 = c1 + Q_LORA
    c3 = c2 + KV_LORA
    val = p[:, :c0]
    gat = p[:, c0:c1]
    u_ref[...] = val * _sigmoid(gat)
    cqn = (_rms(p[:, c1:c2]) * gcq_ref[...]).astype(BF16)
    ckvn = (_rms(p[:, c2:c3]) * gckv_ref[...]).astype(BF16)
    kpe = p[:, c3:c3 + LANE]
    kpe_sw = p[:, c3 + LANE:c3 + 2 * LANE]
    qa, qb, ka, kb = tab_ref[0], tab_ref[1], tab_ref[2], tab_ref[3]
    kpe_rot = kpe * ka + kpe_sw * kb
    qq = _dot(cqn, wq_ref[...])
    kv = _dot(ckvn, wkv_ref[...])
    hw = MLA_HEADS * LANE
    for hd in range(MLA_HEADS):
        lo, hi = hd * LANE, (hd + 1) * LANE
        q_ref[:, lo:hi] = (qq[:, lo:hi] * qa + qq[:, hw + lo:hw + hi] * qb).astype(BF16)
        k_ref[:, lo:hi] = (kv[:, lo:hi] + kpe_rot).astype(BF16)
    v_ref[...] = kv[:, hw:].astype(BF16)


def _even_in(x, mod, g_pre, win, gcq, wq, gckv, wkv, tabs, rows_per_batch, fixed_row=None):
    r, d = x.shape
    nd = mod.shape[-1]
    tm = _row_tile(rows_per_batch, r)
    per = rows_per_batch // tm
    row = lambda t: (t, 0)
    hw = MLA_HEADS * LANE
    if fixed_row is None:
        tab_spec = pl.BlockSpec((4, tm, LANE), lambda t: (0, t % per, 0))
    else:
        tab_spec = pl.BlockSpec((4, tm, LANE), lambda t: (0, 0, 0))
    return pl.pallas_call(
        _even_in_kernel,
        grid=(r // tm,),
        in_specs=[pl.BlockSpec((tm, d), row),
                  _mod_spec(nd, rows_per_batch, tm, fixed_row),
                  pl.BlockSpec((1, d), lambda t: (0, 0)),
                  _resident(win.shape),
                  pl.BlockSpec((1, Q_LORA), lambda t: (0, 0)),
                  _resident(wq.shape),
                  pl.BlockSpec((1, KV_LORA), lambda t: (0, 0)),
                  _resident(wkv.shape),
                  tab_spec],
        out_specs=[pl.BlockSpec((tm, CONV_CH), row), pl.BlockSpec((tm, hw), row),
                   pl.BlockSpec((tm, hw), row), pl.BlockSpec((tm, hw), row)],
        out_shape=[jax.ShapeDtypeStruct((r, CONV_CH), F32), jax.ShapeDtypeStruct((r, hw), BF16),
                   jax.ShapeDtypeStruct((r, hw), BF16), jax.ShapeDtypeStruct((r, hw), BF16)],
        compiler_params=_cparams(1),
        name="even_in",
    )(x, mod, g_pre.reshape(1, d), win, gcq.reshape(1, -1), wq, gckv.reshape(1, -1), wkv, tabs)


def _head_mean_sq(x, g_ref):
    sq = x * x
    hi = sq.astype(BF16)
    lo = (sq - hi.astype(F32)).astype(BF16)
    return _dot(hi, g_ref[...]) + _dot(lo, g_ref[...])


def _odd_in_kernel(x_ref, mod_ref, gpre_ref, win_ref, gmat_ref, dft_ref, tab_ref,
                   q_ref, k_ref, v_ref, f_ref):
    x = x_ref[...]
    d = x.shape[-1]
    shift = mod_ref[:, 3 * d:4 * d]
    scale = mod_ref[:, 4 * d:5 * d]
    h = (_rms(x) * gpre_ref[...] * (1.0 + scale) + shift).astype(BF16)
    p = _dot(h, win_ref[...])
    nq = GQA_HEADS * GQA_HD
    nk = GQA_KV_HEADS * LANE
    o_qs = nq
    o_k = 2 * nq
    o_ks = o_k + nk
    o_v = o_ks + nk
    o_f = o_v + 2 * nk
    qa, qb, ka, kb = tab_ref[0], tab_ref[1], tab_ref[2], tab_ref[3]
    for c in range(nq // MXU_DIM):
        lo = c * MXU_DIM
        qc = p[:, lo:lo + MXU_DIM]
        qs = p[:, o_qs + lo:o_qs + lo + MXU_DIM]
        r = lax.rsqrt(_head_mean_sq(qc, gmat_ref) + EPS)
        for s in range(MXU_DIM // LANE):
            a, b = s * LANE, (s + 1) * LANE
            q_ref[:, lo + a:lo + b] = (r[:, a:b] * (qc[:, a:b] * qa + qs[:, a:b] * qb)).astype(BF16)
    kc = p[:, o_k:o_k + nk]
    ks = p[:, o_ks:o_ks + nk]
    r = lax.rsqrt(_head_mean_sq(kc, gmat_ref) + EPS)
    for s in range(nk // LANE):
        a, b = s * LANE, (s + 1) * LANE
        k_ref[:, a:b] = (r[:, a:b] * (kc[:, a:b] * ka + ks[:, a:b] * kb)).astype(BF16)
    v_ref[...] = p[:, o_v:o_f].astype(BF16)
    nf = FNET_GROUPS * FNET_CH
    for c in range(nf // MXU_DIM):
        lo = c * MXU_DIM
        fx = p[:, o_f + lo:o_f + lo + MXU_DIM].astype(BF16)
        cs = _dot(fx, dft_ref[...])
        f_ref[:, lo:lo + MXU_DIM] = cs[:, :MXU_DIM].astype(BF16)
        f_ref[:, nf + lo:nf + lo + MXU_DIM] = cs[:, MXU_DIM:].astype(BF16)


def _odd_in(x, mod, g_pre, win, gmat, dft, tabs, rows_per_batch, fixed_row=None):
    r, d = x.shape
    nd = mod.shape[-1]
    tm = _row_tile(rows_per_batch, r)
    per = rows_per_batch // tm
    row = lambda t: (t, 0)
    nq = GQA_HEADS * GQA_HD
    nk = GQA_KV_HEADS * LANE
    nf = FNET_GROUPS * FNET_CH
    if fixed_row is None:
        tab_spec = pl.BlockSpec((4, tm, LANE), lambda t: (0, t % per, 0))
    else:
        tab_spec = pl.BlockSpec((4, tm, LANE), lambda t: (0, 0, 0))
    return pl.pallas_call(
        _odd_in_kernel,
        grid=(r // tm,),
        in_specs=[pl.BlockSpec((tm, d), row),
                  _mod_spec(nd, rows_per_batch, tm, fixed_row),
                  pl.BlockSpec((1, d), lambda t: (0, 0)),
                  _resident(win.shape),
                  _resident(gmat.shape),
                  _resident(dft.shape),
                  tab_spec],
        out_specs=[pl.BlockSpec((tm, nq), row), pl.BlockSpec((tm, nk), row),
                   pl.BlockSpec((tm, 2 * nk), row), pl.BlockSpec((tm, 2 * nf), row)],
        out_shape=[jax.ShapeDtypeStruct((r, nq), BF16), jax.ShapeDtypeStruct((r, nk), BF16),
                   jax.ShapeDtypeStruct((r, 2 * nk), BF16), jax.ShapeDtypeStruct((r, 2 * nf), BF16)],
        compiler_params=_cparams(1),
        name="odd_in",
    )(x, mod, g_pre.reshape(1, d), win, gmat, dft, tabs)


def _attn_kernel(*refs, n_src, q_slab, q_half, k_slab, v_slab):
    q_ref = refs[0]
    k_refs = refs[1:1 + n_src]
    v_refs = refs[1 + n_src:1 + 2 * n_src]
    o_ref = refs[1 + 2 * n_src]
    tq = q_ref.shape[0]
    lane = lax.broadcasted_iota(jnp.int32, (tq, LANE), 1)
    acc = [None] * (HEADS // 2)
    for hd in range(HEADS):
        q = q_ref[:, q_slab[hd] * LANE:(q_slab[hd] + 1) * LANE]
        if q_half[hd] is not None:
            keep = (lane < HEAD_V) if q_half[hd] == 0 else (lane >= HEAD_V)
            q = jnp.where(keep, q, jnp.zeros_like(q))
        ks, vs = k_slab[hd], v_slab[hd]
        s = [_dot_t(q, kr[:, ks * LANE:(ks + 1) * LANE]) for kr in k_refs]
        m = s[0].max(axis=-1, keepdims=True)
        for si in s[1:]:
            m = jnp.maximum(m, si.max(axis=-1, keepdims=True))
        p = [jnp.exp(si - m) for si in s]
        den = p[0].sum(axis=-1, keepdims=True)
        for pi in p[1:]:
            den = den + pi.sum(axis=-1, keepdims=True)
        o = _dot(p[0].astype(BF16), v_refs[0][:, vs * LANE:(vs + 1) * LANE])
        for pi, vr in zip(p[1:], v_refs[1:]):
            o = o + _dot(pi.astype(BF16), vr[:, vs * LANE:(vs + 1) * LANE])
        o = o / den
        acc[hd // 2] = o if acc[hd // 2] is None else acc[hd // 2] + o
    for j in range(HEADS // 2):
        o_ref[:, j * LANE:(j + 1) * LANE] = acc[j].astype(BF16)


def _attention(q, kvs, n_batch, lq, head_maps, tq=256):
    tq = min(tq, lq)
    per = lq // tq
    n_src = len(kvs)
    wq = q.shape[1]
    in_specs = [pl.BlockSpec((tq, wq), lambda b, i: (b * per + i, 0))]
    for k, _, lk in kvs:
        in_specs.append(pl.BlockSpec((lk, k.shape[1]), lambda b, i: (b, 0)))
    for _, v, lk in kvs:
        in_specs.append(pl.BlockSpec((lk, v.shape[1]), lambda b, i: (b, 0)))
    kern = functools.partial(_attn_kernel, n_src=n_src, **head_maps)
    return pl.pallas_call(
        kern,
        grid=(n_batch, per),
        in_specs=in_specs,
        out_specs=pl.BlockSpec((tq, HEADS * HEAD_V), lambda b, i: (b * per + i, 0)),
        out_shape=jax.ShapeDtypeStruct((n_batch * lq, HEADS * HEAD_V), BF16),
        compiler_params=_cparams(2),
        name="attention",
    )(q, *[k for k, _, _ in kvs], *[v for _, v, _ in kvs])


_MLA_MAPS = dict(q_slab=tuple(range(8)), q_half=(None,) * 8,
                 k_slab=tuple(range(8)), v_slab=tuple(range(8)))
_GQA_MAPS = dict(q_slab=tuple(h // 2 for h in range(8)), q_half=tuple(h % 2 for h in range(8)),
                 k_slab=tuple(h // GQA_GROUP for h in range(8)),
                 v_slab=tuple(2 * (h // GQA_GROUP) + h % 2 for h in range(8)))


def _conv_kernel(u_ref, w_ref, b_ref, g_ref, beta_ref, o_ref, pad_ref):
    n, ch = u_ref.shape
    zeros = jnp.zeros((CONV_HALO, ch), F32)
    pad_ref[0:CONV_HALO, :] = zeros
    pad_ref[CONV_HALO + n:2 * CONV_HALO + n, :] = zeros
    pad_ref[CONV_HALO:CONV_HALO + n, :] = u_ref[...]
    first = CONV_HALO - CONV_K // 2
    span = CONV_ROWS + 2 * CONV_HALO - SUBLANE

    def step(i, carry):
        r0 = pl.multiple_of(i * CONV_ROWS, CONV_ROWS)
        win = pad_ref[pl.ds(r0, CONV_ROWS + 2 * CONV_HALO), :]
        shifted = [win[s:s + span, :] for s in range(SUBLANE)]
        acc = jnp.zeros((CONV_ROWS, ch), F32)
        for k in range(CONV_K):
            a, s = divmod(first + k, SUBLANE)
            acc = acc + shifted[s][a * SUBLANE:a * SUBLANE + CONV_ROWS, :] * w_ref[k:k + 1, :]
        v = acc + b_ref[...]
        mu = jnp.mean(v, axis=-1, keepdims=True)
        cen = v - mu
        var = jnp.mean(cen * cen, axis=-1, keepdims=True)
        y = cen * lax.rsqrt(var + EPS) * g_ref[...] + beta_ref[...]
        o_ref[pl.ds(r0, CONV_ROWS), :] = (y * _sigmoid(y)).astype(BF16)
        return carry

    lax.fori_loop(0, n // CONV_ROWS, step, 0)


def _conformer_conv(u, n_batch, n, conv_w, conv_b, ln_g, ln_b):
    ch = u.shape[-1]
    vec = pl.BlockSpec((1, ch), lambda b: (0, 0))
    return pl.pallas_call(
        _conv_kernel,
        grid=(n_batch,),
        in_specs=[pl.BlockSpec((n, ch), lambda b: (b, 0)),
                  pl.BlockSpec((CONV_K, ch), lambda b: (0, 0)), vec, vec, vec],
        out_specs=pl.BlockSpec((n, ch), lambda b: (b, 0)),
        out_shape=jax.ShapeDtypeStruct((n_batch * n, ch), BF16),
        scratch_shapes=[pltpu.VMEM((n + 2 * CONV_HALO, ch), F32)],
        compiler_params=_cparams(1),
        name="conformer_conv",
    )(u, conv_w, conv_b.reshape(1, ch), ln_g.reshape(1, ch), ln_b.reshape(1, ch))


def _seq_dft_kernel(x_ref, c_ref, s_ref, o_ref, *, norm):
    tr = o_ref.shape[0]
    nf = o_ref.shape[1]
    r0 = pl.multiple_of(pl.program_id(1) * tr, tr)
    acc = _dot(c_ref[pl.ds(r0, tr), :], x_ref[:, :nf]) - _dot(s_ref[pl.ds(r0, tr), :], x_ref[:, nf:])
    o_ref[...] = (acc * norm).astype(BF16)


def _seq_dft(xf, n_batch, n, cmat, smat):
    nf = xf.shape[1] // 2
    tr = min(512, n)
    per = n // tr
    norm = 1.0 / math.sqrt(n * FNET_CH)
    return pl.pallas_call(
        functools.partial(_seq_dft_kernel, norm=norm),
        grid=(n_batch, per),
        in_specs=[pl.BlockSpec((n, 2 * nf), lambda b, j: (b, 0)),
                  _resident(cmat.shape), _resident(smat.shape)],
        out_specs=pl.BlockSpec((tr, nf), lambda b, j: (b * per + j, 0)),
        out_shape=jax.ShapeDtypeStruct((n_batch * n, nf), BF16),
        compiler_params=_cparams(2),
        name="seq_dft",
    )(xf, cmat, smat)


def _rope_cos_sin(n, rot_dim):
    nf = rot_dim // 4
    t = jnp.arange(n)
    row = (t // GRID_W).astype(F32)
    col = (t % GRID_W).astype(F32)
    inv = ROPE_THETA ** (-jnp.arange(nf, dtype=F32) / nf)
    ang = jnp.concatenate([row[:, None] * inv, col[:, None] * inv], axis=-1)
    cos, sin = jnp.cos(ang), jnp.sin(ang)
    lanes = np.arange(rot_dim)
    half = (lanes % (2 * nf)) // nf
    src = (lanes // (2 * nf)) * nf + lanes % nf
    sign = np.where(half == 0, -1.0, 1.0).astype(np.float32)
    partner = np.where(half == 0, lanes + nf, lanes - nf)
    return cos[:, src], sin[:, src] * sign, partner


def _mla_tables(n, tm_ctx):
    cos, sin, partner = _rope_cos_sin(n, MLA_ROPE)
    ones = jnp.ones((n, MLA_NOPE), F32)
    zer = jnp.zeros((n, MLA_NOPE), F32)
    pad = jnp.zeros((n, LANE - MLA_NOPE - MLA_ROPE), F32)
    qa = jnp.concatenate([ones, cos, pad], axis=1) * MLA_SCALE
    qb = jnp.concatenate([zer, sin, pad], axis=1) * MLA_SCALE
    ka = jnp.concatenate([zer, cos, pad], axis=1)
    kb = jnp.concatenate([zer, sin, pad], axis=1)
    lane = np.arange(LANE)
    c_qa = np.where(lane < MLA_NOPE + MLA_ROPE, MLA_SCALE, 0.0).astype(np.float32)
    c_ka = np.where((lane >= MLA_NOPE) & (lane < MLA_NOPE + MLA_ROPE), 1.0, 0.0).astype(np.float32)
    ctx = [jnp.broadcast_to(jnp.asarray(c), (tm_ctx, LANE))
           for c in (c_qa, np.zeros(LANE, np.float32), c_ka, np.zeros(LANE, np.float32))]
    lat = jnp.stack([qa, qb, ka, kb])
    return lat, jnp.stack(ctx), partner


def _gqa_tables(n, tm_ctx, g_qn, g_kn):
    cos, sin, partner = _rope_cos_sin(n, GQA_HD)
    two = lambda a: jnp.concatenate([a, a], axis=-1)
    qa = two(cos * g_qn) * GQA_SCALE
    qb = two(sin * g_qn[partner]) * GQA_SCALE
    ka = two(cos * g_kn)
    kb = two(sin * g_kn[partner])
    lat = jnp.stack([qa, qb, ka, kb])
    zero = jnp.zeros((tm_ctx, LANE), F32)
    ctx = jnp.stack([jnp.broadcast_to(two(g_qn) * GQA_SCALE, (tm_ctx, LANE)), zero,
                     jnp.broadcast_to(two(g_kn), (tm_ctx, LANE)), zero])
    return lat, ctx, partner


def _even_weights(w_in, w_uq, w_ukv, partner):
    d = w_in.shape[0]
    a_cols = 2 * CONV_CH
    kpe0 = a_cols + Q_LORA + KV_LORA
    rope_lo = MLA_NOPE
    zpad = lambda n: jnp.zeros((d, n), F32)
    kpe = w_in[:, kpe0:kpe0 + MLA_ROPE]
    kpe_slab = jnp.concatenate([zpad(rope_lo), kpe, zpad(LANE - rope_lo - MLA_ROPE)], axis=1)
    kpe_sw_slab = jnp.concatenate([zpad(rope_lo), kpe[:, partner],
                                   zpad(LANE - rope_lo - MLA_ROPE)], axis=1)
    win = jnp.concatenate([w_in[:, :kpe0], kpe_slab, kpe_sw_slab], axis=1).astype(BF16)

    dk = MLA_NOPE + MLA_ROPE
    wq3 = w_uq.reshape(Q_LORA, MLA_HEADS, dk)
    qz = jnp.zeros((Q_LORA, MLA_HEADS, LANE - dk), F32)
    wq_pad = jnp.concatenate([wq3, qz], axis=2)
    wq_sw = jnp.concatenate([jnp.zeros((Q_LORA, MLA_HEADS, MLA_NOPE), F32),
                             wq3[:, :, MLA_NOPE:][:, :, partner], qz], axis=2)
    wq = jnp.concatenate([wq_pad.reshape(Q_LORA, -1), wq_sw.reshape(Q_LORA, -1)], axis=1).astype(BF16)

    wkv3 = w_ukv.reshape(KV_LORA, MLA_HEADS, MLA_NOPE + MLA_V)
    kz = jnp.zeros((KV_LORA, MLA_HEADS, LANE - MLA_NOPE), F32)
    wk_pad = jnp.concatenate([wkv3[:, :, :MLA_NOPE], kz], axis=2)
    wv = wkv3[:, :, MLA_NOPE:]
    vz = jnp.zeros_like(wv)
    even = (np.arange(MLA_HEADS) % 2 == 0)[None, :, None]
    wv_pad = jnp.concatenate([jnp.where(even, wv, vz), jnp.where(even, vz, wv)], axis=2)
    wkv = jnp.concatenate([wk_pad.reshape(KV_LORA, -1), wv_pad.reshape(KV_LORA, -1)],
                          axis=1).astype(BF16)
    return win, wq, wkv


def _odd_weights(w_in, partner):
    d = w_in.shape[0]
    nq = GQA_HEADS * GQA_HD
    nkv = GQA_KV_HEADS * GQA_HD
    wq = w_in[:, :nq].reshape(d, GQA_HEADS, GQA_HD)
    wk = w_in[:, nq:nq + nkv].reshape(d, GQA_KV_HEADS, GQA_HD)
    wv = w_in[:, nq + nkv:nq + 2 * nkv].reshape(d, GQA_KV_HEADS, GQA_HD)
    wf = w_in[:, nq + 2 * nkv:]
    dup = lambda a: jnp.concatenate([a, a], axis=2).reshape(d, -1)
    vz = jnp.zeros_like(wv)
    v4 = jnp.stack([jnp.concatenate([wv, vz], axis=2), jnp.concatenate([vz, wv], axis=2)],
                   axis=2).reshape(d, -1)
    return jnp.concatenate([wq.reshape(d, -1), wq[:, :, partner].reshape(d, -1),
                            dup(wk), dup(wk[:, :, partner]), v4, wf], axis=1).astype(BF16)


def _head_mean_matrix():
    i = np.arange(MXU_DIM)
    same = (i[:, None] // GQA_HD) == (i[None, :] // GQA_HD)
    return jnp.asarray(np.where(same, 1.0 / GQA_HD, 0.0), BF16)


def _channel_dft_matrix():
    i = np.arange(MXU_DIM)
    ang = 2.0 * np.pi * ((i[:, None] % FNET_CH) * (i[None, :] % FNET_CH) % FNET_CH) / FNET_CH
    same = (i[:, None] // FNET_CH) == (i[None, :] // FNET_CH)
    c = np.where(same, np.cos(ang), 0.0)
    s = np.where(same, np.sin(ang), 0.0)
    return jnp.asarray(np.concatenate([c, s], axis=1), BF16)


def _seq_dft_matrices(n):
    i = jnp.arange(n, dtype=jnp.int32)
    ang = ((i[:, None] * i[None, :]) % n).astype(F32) * (2.0 * math.pi / n)
    return jnp.cos(ang).astype(BF16), jnp.sin(ang).astype(BF16)


def _ffn_weights(w_gate, w_up, w_down):
    d, dff = w_gate.shape
    n_chunks = dff // FF_CHUNK
    wg = w_gate.reshape(d, n_chunks, 1, FF_CHUNK)
    wu = w_up.reshape(d, n_chunks, 1, FF_CHUNK)
    wgu = jnp.concatenate([wg, wu], axis=2).transpose(1, 0, 2, 3).reshape(n_chunks, d, 2 * FF_CHUNK)
    return wgu.astype(BF16), w_down.astype(BF16)


def kernel(x, c, ctx, c_ctx, w_mod, b_mod, g_pre, g_post, ffn_w_gate, ffn_w_up, ffn_w_down,
           ev_w_in, ev_conv_w, ev_conv_b, ev_ln_g, ev_ln_b, ev_g_cq, ev_w_uq, ev_g_ckv,
           ev_w_ukv, ev_w_out, od_w_in, od_g_qn, od_g_kn, od_w_out):
    n_b, n_l, d = x.shape
    n_c = ctx.shape[1]
    depth = w_mod.shape[0]
    nd = w_mod.shape[2]

    bp = -(-(n_b + 1) // SUBLANE) * SUBLANE
    c_all = jnp.concatenate([c, c_ctx[None, :], jnp.zeros((bp - n_b - 1, d), F32)], axis=0)
    mod_all = _modulation(c_all, w_mod, b_mod).reshape(depth, bp, 1, nd)

    xl = x.reshape(n_b * n_l, d)
    xc = ctx.reshape(n_b * n_c, d)
    ctx_rows = n_b * n_c
    tm_c = _row_tile(ctx_rows, ctx_rows)

    mla_lat, mla_ctx, mla_partner = _mla_tables(n_l, tm_c)
    gmat = _head_mean_matrix()
    dft_ch = _channel_dft_matrix()
    cmat_l, smat_l = _seq_dft_matrices(n_l)
    cmat_c, smat_c = _seq_dft_matrices(n_c)

    for i in range(depth):
        need_ctx = i < depth - 1
        mod = mod_all[i]
        j = i // 2
        w0 = _ffn_weights(ffn_w_gate[i, 0], ffn_w_up[i, 0], ffn_w_down[i, 0])
        w1 = _ffn_weights(ffn_w_gate[i, 1], ffn_w_up[i, 1], ffn_w_down[i, 1])
        xl = _ffn(xl, mod, 0, g_pre[i, 0], g_post[i, 0], *w0, rows_per_batch=n_l)
        xc = _ffn(xc, mod, 0, g_pre[i, 0], g_post[i, 0], *w0, rows_per_batch=ctx_rows, fixed_row=n_b)
        if i % 2 == 0:
            win, wq, wkv = _even_weights(ev_w_in[j], ev_w_uq[j], ev_w_ukv[j], mla_partner)
            proj = (g_pre[i, 1], win, ev_g_cq[j], wq, ev_g_ckv[j], wkv)
            u_l, q_l, k_l, v_l = _even_in(xl, mod, *proj, mla_lat, rows_per_batch=n_l)
            u_c, q_c, k_c, v_c = _even_in(xc, mod, *proj, mla_ctx, rows_per_batch=ctx_rows,
                                          fixed_row=n_b)
            conv = (ev_conv_w[j], ev_conv_b[j], ev_ln_g[j], ev_ln_b[j])
            ob_l = _attention(q_l, [(k_c, v_c, n_c), (k_l, v_l, n_l)], n_b, n_l, _MLA_MAPS)
            oa_l = _conformer_conv(u_l, n_b, n_l, *conv)
            if need_ctx:
                ob_c = _attention(q_c, [(k_c, v_c, n_c)], n_b, n_c, _MLA_MAPS)
                oa_c = _conformer_conv(u_c, n_b, n_c, *conv)
            w_out = ev_w_out[j].astype(BF16)
        else:
            gqa_lat, gqa_ctx, gqa_partner = _gqa_tables(n_l, tm_c, od_g_qn[j], od_g_kn[j])
            win = _odd_weights(od_w_in[j], gqa_partner)
            proj = (g_pre[i, 1], win, gmat, dft_ch)
            q_l, k_l, v_l, f_l = _odd_in(xl, mod, *proj, gqa_lat, rows_per_batch=n_l)
            q_c, k_c, v_c, f_c = _odd_in(xc, mod, *proj, gqa_ctx, rows_per_batch=ctx_rows,
                                         fixed_row=n_b)
            oa_l = _attention(q_l, [(k_c, v_c, n_c), (k_l, v_l, n_l)], n_b, n_l, _GQA_MAPS)
            ob_l = _seq_dft(f_l, n_b, n_l, cmat_l, smat_l)
            if need_ctx:
                oa_c = _attention(q_c, [(k_c, v_c, n_c)], n_b, n_c, _GQA_MAPS)
                ob_c = _seq_dft(f_c, n_b, n_c, cmat_c, smat_c)
            w_out = od_w_out[j].astype(BF16)
        xl = _ffn(xl, mod, 2, g_pre[i, 2], g_post[i, 2], *w1, rows_per_batch=n_l,
                  mix=(oa_l, ob_l, w_out, g_post[i, 1]))
        if need_ctx:
            xc = _ffn(xc, mod, 2, g_pre[i, 2], g_post[i, 2], *w1, rows_per_batch=ctx_rows,
                      fixed_row=n_b, mix=(oa_c, ob_c, w_out, g_post[i, 1]))
    return xl.reshape(n_b, n_l, d)
```

```python
import functools
import math

import numpy as np
import jax
import jax.numpy as jnp
from jax import lax
from jax.experimental import pallas as pl
from jax.experimental.pallas import tpu as pltpu

F32 = jnp.float32
BF16 = jnp.bfloat16

N_MOD = 9
FFN_RES = 0.5
EPS = 1e-6
ROPE_THETA = 10000.0
GRID_W = 64
CONV_CH = 512
CONV_K = 31
MLA_HEADS = 8
MLA_NOPE = 64
MLA_ROPE = 32
MLA_V = 64
Q_LORA = 384
KV_LORA = 256
LOG2E = math.log2(math.e)
MLA_SCALE = (MLA_NOPE + MLA_ROPE) ** -0.5 * LOG2E
GQA_HEADS = 8
GQA_KV_HEADS = 2
GQA_GROUP = GQA_HEADS // GQA_KV_HEADS
GQA_HD = 64
GQA_SCALE = GQA_HD ** -0.5 * LOG2E
FNET_GROUPS = 4
FNET_CH = 128
HEADS = 8
HEAD_V = 64

LANE = 128
SUBLANE = 8
MXU_DIM = 256
VMEM_LIMIT = 56 * 1024 * 1024
FF_CHUNK = 256
CONV_HALO = 16
CONV_ROWS = 64
ATTN_SUB = 256


def _cparams(n_axes):
    return pltpu.CompilerParams(dimension_semantics=("arbitrary",) * n_axes,
                                vmem_limit_bytes=VMEM_LIMIT)


def _resident(shape):
    zeros = (0,) * len(shape)
    return pl.BlockSpec(shape, lambda *_: zeros, pipeline_mode=pl.Buffered(1))


def _rms(x):
    return x * lax.rsqrt(jnp.mean(x * x, axis=-1, keepdims=True) + EPS)


def _sigmoid(x):
    return 1.0 / (1.0 + jnp.exp(-x))


def _dot(a, b):
    return jnp.dot(a, b, preferred_element_type=F32)


def _dot_t(a, b):
    return lax.dot_general(a, b, (((1,), (1,)), ((), ())), preferred_element_type=F32)


def _mod_kernel(c_ref, w_ref, b_ref, o_ref):
    c = c_ref[...]
    sc = (c * _sigmoid(c)).astype(BF16)
    o_ref[...] = _dot(sc, w_ref[...].astype(BF16)) + b_ref[...]


def _modulation(c_all, w_mod, b_mod):
    depth, d, nd = w_mod.shape
    bp = c_all.shape[0]
    tn = 1024
    return pl.pallas_call(
        _mod_kernel,
        grid=(depth, nd // tn),
        in_specs=[pl.BlockSpec((bp, d), lambda i, j: (0, 0)),
                  pl.BlockSpec((None, d, tn), lambda i, j: (i, 0, j)),
                  pl.BlockSpec((None, 1, tn), lambda i, j: (i, 0, j))],
        out_specs=pl.BlockSpec((None, bp, tn), lambda i, j: (i, 0, j)),
        out_shape=jax.ShapeDtypeStruct((depth, bp, nd), F32),
        compiler_params=_cparams(2),
        name="modulation",
    )(c_all, w_mod, b_mod.reshape(depth, 1, nd))


def _ffn_body(x, mod_ref, k, gpre_ref, gpost_ref, wg_ref, wu_ref, wd_ref, h_ref, u_ref):
    d = x.shape[-1]
    shift = mod_ref[:, (3 * k) * d:(3 * k + 1) * d]
    scale = mod_ref[:, (3 * k + 1) * d:(3 * k + 2) * d]
    gate = mod_ref[:, (3 * k + 2) * d:(3 * k + 3) * d]
    h = _rms(x) * gpre_ref[...] * (1.0 + scale) + shift
    h_ref[...] = h.astype(BF16)
    ch = FF_CHUNK
    for c in range(wg_ref.shape[1] // ch):
        a = _dot(h_ref[...], wg_ref[:, c * ch:(c + 1) * ch])
        b = _dot(h_ref[...], wu_ref[:, c * ch:(c + 1) * ch])
        u_ref[:, c * ch:(c + 1) * ch] = (a * _sigmoid(a) * b).astype(BF16)
    y = _dot(u_ref[...], wd_ref[...])
    return x + (FFN_RES * gate) * (_rms(y) * gpost_ref[...])


def _ffn_kernel(x_ref, mod_ref, gpre_ref, gpost_ref, wg_ref, wu_ref, wd_ref, o_ref, h_ref, u_ref,
                *, k):
    o_ref[...] = _ffn_body(x_ref[...], mod_ref, k, gpre_ref, gpost_ref, wg_ref, wu_ref, wd_ref,
                           h_ref, u_ref)


def _mix_ffn_kernel(x_ref, oa_ref, ob_ref, wo_ref, gmix_ref, mod_ref, gpre_ref, gpost_ref,
                    wg_ref, wu_ref, wd_ref, o_ref, h_ref, u_ref):
    x = x_ref[...]
    d = x.shape[-1]
    half = oa_ref.shape[-1]
    y = _dot(oa_ref[...], wo_ref[:half, :]) + _dot(ob_ref[...], wo_ref[half:, :])
    gate = mod_ref[:, 5 * d:6 * d]
    x1 = x + gate * (_rms(y) * gmix_ref[...])
    o_ref[...] = _ffn_body(x1, mod_ref, 2, gpre_ref, gpost_ref, wg_ref, wu_ref, wd_ref,
                           h_ref, u_ref)


def _row_tile(rows_per_batch, total_rows, want=512):
    tm = min(want, rows_per_batch)
    assert rows_per_batch % tm == 0 and total_rows % tm == 0
    return tm


def _mod_spec(nd, rows_per_batch, tm, fixed_row):
    if fixed_row is not None:
        return pl.BlockSpec((None, 1, nd), lambda t: (fixed_row, 0, 0))
    per = rows_per_batch // tm
    return pl.BlockSpec((None, 1, nd), lambda t: (t // per, 0, 0))


def _layer_weight(w, i, j):
    return pl.BlockSpec((None, None) + w.shape[2:], lambda *_: (i, j, 0, 0),
                        pipeline_mode=pl.Buffered(1))


def _ffn(x, mod, k, g_pre, g_post, ffn_w, ij, rows_per_batch, fixed_row=None, mix=None):
    r, d = x.shape
    nd = mod.shape[-1]
    wg, wu, wd = ffn_w
    dff = wd.shape[2]
    tm = _row_tile(rows_per_batch, r)
    row = lambda t: (t, 0)
    vec = pl.BlockSpec((1, d), lambda t: (0, 0))
    tail_specs = [_mod_spec(nd, rows_per_batch, tm, fixed_row), vec, vec,
                  _layer_weight(wg, *ij), _layer_weight(wu, *ij), _layer_weight(wd, *ij)]
    tail_args = [mod, g_pre.reshape(1, d), g_post.reshape(1, d), wg, wu, wd]
    scratch = [pltpu.VMEM((tm, d), BF16), pltpu.VMEM((tm, dff), BF16)]
    if mix is None:
        kern = functools.partial(_ffn_kernel, k=k)
        in_specs = [pl.BlockSpec((tm, d), row)] + tail_specs
        args = [x] + tail_args
        name = "ffn"
    else:
        oa, ob, wo, g_mix = mix
        half = oa.shape[-1]
        kern = _mix_ffn_kernel
        in_specs = [pl.BlockSpec((tm, d), row), pl.BlockSpec((tm, half), row),
                    pl.BlockSpec((tm, half), row), _resident(wo.shape), vec] + tail_specs
        args = [x, oa, ob, wo, g_mix.reshape(1, d)] + tail_args
        name = "mix_ffn"
    return pl.pallas_call(
        kern,
        grid=(r // tm,),
        in_specs=in_specs,
        out_specs=pl.BlockSpec((tm, d), row),
        out_shape=jax.ShapeDtypeStruct((r, d), F32),
        scratch_shapes=scratch,
        compiler_params=_cparams(1),
        name=name,
    )(*args)


def _even_in_kernel(x_ref, mod_ref, gpre_ref, win_ref, gcq_ref, wq_ref, gckv_ref, wkv_ref, tab_ref,
                    u_ref, q_ref, k_ref, v_ref):
    x = x_ref[...]
    d = x.shape[-1]
    shift = mod_ref[:, 3 * d:4 * d]
    scale = mod_ref[:, 4 * d:5 * d]
    h = (_rms(x) * gpre_ref[...] * (1.0 + scale) + shift).astype(BF16)
    p = _dot(h, win_ref[...])
    c0 = CONV_CH
    c1 = 2 * CONV_CH
    c2 = c1 + Q_LORA
    c3 = c2 + KV_LORA
    val = p[:, :c0]
    gat = p[:, c0:c1]
    u_ref[...] = val * _sigmoid(gat)
    cqn = (_rms(p[:, c1:c2]) * gcq_ref[...]).astype(BF16)
    ckvn = (_rms(p[:, c2:c3]) * gckv_ref[...]).astype(BF16)
    kpe = p[:, c3:c3 + LANE]
    kpe_sw = p[:, c3 + LANE:c3 + 2 * LANE]
    qa, qb, ka, kb = tab_ref[0], tab_ref[1], tab_ref[2], tab_ref[3]
    kpe_rot = kpe * ka + kpe_sw * kb
    qq = _dot(cqn, wq_ref[...])
    kv = _dot(ckvn, wkv_ref[...])
    hw = MLA_HEADS * LANE
    for hd in range(MLA_HEADS):
        lo, hi = hd * LANE, (hd + 1) * LANE
        q_ref[:, lo:hi] = (qq[:, lo:hi] * qa + qq[:, hw + lo:hw + hi] * qb).astype(BF16)
        k_ref[:, lo:hi] = (kv[:, lo:hi] + kpe_rot).astype(BF16)
    v_ref[...] = kv[:, hw:].astype(BF16)


def _even_in(x, mod, g_pre, win, gcq, wq, gckv, wkv, tabs, rows_per_batch, fixed_row=None):
    r, d = x.shape
    nd = mod.shape[-1]
    tm = _row_tile(rows_per_batch, r)
    per = rows_per_batch // tm
    row = lambda t: (t, 0)
    hw = MLA_HEADS * LANE
    if fixed_row is None:
        tab_spec = pl.BlockSpec((4, tm, LANE), lambda t: (0, t % per, 0))
    else:
        tab_spec = pl.BlockSpec((4, tm, LANE), lambda t: (0, 0, 0))
    return pl.pallas_call(
        _even_in_kernel,
        grid=(r // tm,),
        in_specs=[pl.BlockSpec((tm, d), row),
                  _mod_spec(nd, rows_per_batch, tm, fixed_row),
                  pl.BlockSpec((1, d), lambda t: (0, 0)),
                  _resident(win.shape),
                  pl.BlockSpec((1, Q_LORA), lambda t: (0, 0)),
                  _resident(wq.shape),
                  pl.BlockSpec((1, KV_LORA), lambda t: (0, 0)),
                  _resident(wkv.shape),
                  tab_spec],
        out_specs=[pl.BlockSpec((tm, CONV_CH), row), pl.BlockSpec((tm, hw), row),
                   pl.BlockSpec((tm, hw), row), pl.BlockSpec((tm, hw), row)],
        out_shape=[jax.ShapeDtypeStruct((r, CONV_CH), F32), jax.ShapeDtypeStruct((r, hw), BF16),
                   jax.ShapeDtypeStruct((r, hw), BF16), jax.ShapeDtypeStruct((r, hw), BF16)],
        compiler_params=_cparams(1),
        name="even_in",
    )(x, mod, g_pre.reshape(1, d), win, gcq.reshape(1, -1), wq, gckv.reshape(1, -1), wkv, tabs)


def _head_mean_sq(x, g_ref):
    sq = x * x
    hi = sq.astype(BF16)
    lo = (sq - hi.astype(F32)).astype(BF16)
    return _dot(hi, g_ref[...]) + _dot(lo, g_ref[...])


def _odd_in_kernel(x_ref, mod_ref, gpre_ref, win_ref, gmat_ref, dft_ref, tab_ref,
                   q_ref, k_ref, v_ref, f_ref):
    x = x_ref[...]
    d = x.shape[-1]
    shift = mod_ref[:, 3 * d:4 * d]
    scale = mod_ref[:, 4 * d:5 * d]
    h = (_rms(x) * gpre_ref[...] * (1.0 + scale) + shift).astype(BF16)
    p = _dot(h, win_ref[...])
    nq = GQA_HEADS * GQA_HD
    nk = GQA_KV_HEADS * LANE
    o_qs = nq
    o_k = 2 * nq
    o_ks = o_k + nk
    o_v = o_ks + nk
    o_f = o_v + 2 * nk
    qa, qb, ka, kb = tab_ref[0], tab_ref[1], tab_ref[2], tab_ref[3]
    for c in range(nq // MXU_DIM):
        lo = c * MXU_DIM
        qc = p[:, lo:lo + MXU_DIM]
        qs = p[:, o_qs + lo:o_qs + lo + MXU_DIM]
        r = lax.rsqrt(_head_mean_sq(qc, gmat_ref) + EPS)
        for s in range(MXU_DIM // LANE):
            a, b = s * LANE, (s + 1) * LANE
            q_ref[:, lo + a:lo + b] = (r[:, a:b] * (qc[:, a:b] * qa + qs[:, a:b] * qb)).astype(BF16)
    kc = p[:, o_k:o_k + nk]
    ks = p[:, o_ks:o_ks + nk]
    r = lax.rsqrt(_head_mean_sq(kc, gmat_ref) + EPS)
    for s in range(nk // LANE):
        a, b = s * LANE, (s + 1) * LANE
        k_ref[:, a:b] = (r[:, a:b] * (kc[:, a:b] * ka + ks[:, a:b] * kb)).astype(BF16)
    v_ref[...] = p[:, o_v:o_f].astype(BF16)
    nf = FNET_GROUPS * FNET_CH
    for c in range(nf // MXU_DIM):
        lo = c * MXU_DIM
        fx = p[:, o_f + lo:o_f + lo + MXU_DIM].astype(BF16)
        cs = _dot(fx, dft_ref[...])
        f_ref[:, lo:lo + MXU_DIM] = cs[:, :MXU_DIM].astype(BF16)
        f_ref[:, nf + lo:nf + lo + MXU_DIM] = cs[:, MXU_DIM:].astype(BF16)


def _odd_in(x, mod, g_pre, win, gmat, dft, tabs, rows_per_batch, fixed_row=None):
    r, d = x.shape
    nd = mod.shape[-1]
    tm = _row_tile(rows_per_batch, r)
    per = rows_per_batch // tm
    row = lambda t: (t, 0)
    nq = GQA_HEADS * GQA_HD
    nk = GQA_KV_HEADS * LANE
    nf = FNET_GROUPS * FNET_CH
    if fixed_row is None:
        tab_spec = pl.BlockSpec((4, tm, LANE), lambda t: (0, t % per, 0))
    else:
        tab_spec = pl.BlockSpec((4, tm, LANE), lambda t: (0, 0, 0))
    return pl.pallas_call(
        _odd_in_kernel,
        grid=(r // tm,),
        in_specs=[pl.BlockSpec((tm, d), row),
                  _mod_spec(nd, rows_per_batch, tm, fixed_row),
                  pl.BlockSpec((1, d), lambda t: (0, 0)),
                  _resident(win.shape),
                  _resident(gmat.shape),
                  _resident(dft.shape),
                  tab_spec],
        out_specs=[pl.BlockSpec((tm, nq), row), pl.BlockSpec((tm, nk), row),
                   pl.BlockSpec((tm, 2 * nk), row), pl.BlockSpec((tm, 2 * nf), row)],
        out_shape=[jax.ShapeDtypeStruct((r, nq), BF16), jax.ShapeDtypeStruct((r, nk), BF16),
                   jax.ShapeDtypeStruct((r, 2 * nk), BF16), jax.ShapeDtypeStruct((r, 2 * nf), BF16)],
        compiler_params=_cparams(1),
        name="odd_in",
    )(x, mod, g_pre.reshape(1, d), win, gmat, dft, tabs)


def _attn_kernel(*refs, n_src, q_slab, q_half, k_slab, v_slab):
    q_ref = refs[0]
    k_refs = refs[1:1 + n_src]
    v_refs = refs[1 + n_src:1 + 2 * n_src]
    o_ref = refs[1 + 2 * n_src]
    tq = q_ref.shape[0]
    sub = min(tq, ATTN_SUB)
    lane = lax.broadcasted_iota(jnp.int32, (sub, LANE), 1)
    for r0 in range(0, tq, sub):
        acc = [None] * (HEADS // 2)
        for hd in range(HEADS):
            q = q_ref[r0:r0 + sub, q_slab[hd] * LANE:(q_slab[hd] + 1) * LANE]
            if q_half[hd] is not None:
                keep = (lane < HEAD_V) if q_half[hd] == 0 else (lane >= HEAD_V)
                q = jnp.where(keep, q, jnp.zeros_like(q))
            ks, vs = k_slab[hd], v_slab[hd]
            s = [_dot_t(q, kr[:, ks * LANE:(ks + 1) * LANE]) for kr in k_refs]
            m = s[0].max(axis=-1, keepdims=True)
            for si in s[1:]:
                m = jnp.maximum(m, si.max(axis=-1, keepdims=True))
            p = [jnp.exp2(si - m) for si in s]
            den = p[0].sum(axis=-1, keepdims=True)
            for pi in p[1:]:
                den = den + pi.sum(axis=-1, keepdims=True)
            o = _dot(p[0].astype(BF16), v_refs[0][:, vs * LANE:(vs + 1) * LANE])
            for pi, vr in zip(p[1:], v_refs[1:]):
                o = o + _dot(pi.astype(BF16), vr[:, vs * LANE:(vs + 1) * LANE])
            o = o / den
            acc[hd // 2] = o if acc[hd // 2] is None else acc[hd // 2] + o
        for j in range(HEADS // 2):
            o_ref[r0:r0 + sub, j * LANE:(j + 1) * LANE] = acc[j].astype(BF16)


def _attention(q, kvs, n_batch, lq, head_maps, tq=256):
    tq = min(tq, lq)
    per = lq // tq
    n_src = len(kvs)
    wq = q.shape[1]
    in_specs = [pl.BlockSpec((tq, wq), lambda b, i: (b * per + i, 0))]
    for k, _, lk in kvs:
        in_specs.append(pl.BlockSpec((lk, k.shape[1]), lambda b, i: (b, 0)))
    for _, v, lk in kvs:
        in_specs.append(pl.BlockSpec((lk, v.shape[1]), lambda b, i: (b, 0)))
    kern = functools.partial(_attn_kernel, n_src=n_src, **head_maps)
    return pl.pallas_call(
        kern,
        grid=(n_batch, per),
        in_specs=in_specs,
        out_specs=pl.BlockSpec((tq, HEADS * HEAD_V), lambda b, i: (b * per + i, 0)),
        out_shape=jax.ShapeDtypeStruct((n_batch * lq, HEADS * HEAD_V), BF16),
        compiler_params=_cparams(2),
        name="attention",
    )(q, *[k for k, _, _ in kvs], *[v for _, v, _ in kvs])


_MLA_MAPS = dict(q_slab=tuple(range(8)), q_half=(None,) * 8,
                 k_slab=tuple(range(8)), v_slab=tuple(range(8)))
_GQA_MAPS = dict(q_slab=tuple(h // 2 for h in range(8)), q_half=tuple(h % 2 for h in range(8)),
                 k_slab=tuple(h // GQA_GROUP for h in range(8)),
                 v_slab=tuple(2 * (h // GQA_GROUP) + h % 2 for h in range(8)))


def _conv_kernel(u_ref, w_ref, b_ref, g_ref, beta_ref, o_ref, pad_ref):
    n, ch = u_ref.shape
    zeros = jnp.zeros((CONV_HALO, ch), F32)
    pad_ref[0:CONV_HALO, :] = zeros
    pad_ref[CONV_HALO + n:2 * CONV_HALO + n, :] = zeros
    pad_ref[CONV_HALO:CONV_HALO + n, :] = u_ref[...]
    first = CONV_HALO - CONV_K // 2
    span = CONV_ROWS + 2 * CONV_HALO

    def step(i, carry):
        r0 = pl.multiple_of(i * CONV_ROWS, CONV_ROWS)
        cols = []
        for c0 in range(0, ch, LANE):
            win = pad_ref[pl.ds(r0, span), c0:c0 + LANE]
            acc = None
            for s in range(SUBLANE):
                sh = win if s == 0 else pltpu.roll(win, span - s, axis=0)
                sh = sh.reshape(span // SUBLANE, SUBLANE, LANE)
                for k in range(CONV_K):
                    a, sk = divmod(first + k, SUBLANE)
                    if sk == s:
                        term = sh[a:a + CONV_ROWS // SUBLANE] * w_ref[k, :, c0:c0 + LANE][None]
                        acc = term if acc is None else acc + term
            cols.append(acc.reshape(CONV_ROWS, LANE) + b_ref[:, c0:c0 + LANE])
        v = jnp.concatenate(cols, axis=1)
        mu = jnp.mean(v, axis=-1, keepdims=True)
        cen = v - mu
        var = jnp.mean(cen * cen, axis=-1, keepdims=True)
        y = cen * lax.rsqrt(var + EPS) * g_ref[...] + beta_ref[...]
        o_ref[pl.ds(r0, CONV_ROWS), :] = (y * _sigmoid(y)).astype(BF16)
        return carry

    lax.fori_loop(0, n // CONV_ROWS, step, 0)


def _conformer_conv(u, n_batch, n, conv_w, conv_b, ln_g, ln_b):
    ch = u.shape[-1]
    vec = pl.BlockSpec((1, ch), lambda b: (0, 0))
    return pl.pallas_call(
        _conv_kernel,
        grid=(n_batch,),
        in_specs=[pl.BlockSpec((n, ch), lambda b: (b, 0)),
                  pl.BlockSpec((CONV_K, SUBLANE, ch), lambda b: (0, 0, 0)), vec, vec, vec],
        out_specs=pl.BlockSpec((n, ch), lambda b: (b, 0)),
        out_shape=jax.ShapeDtypeStruct((n_batch * n, ch), BF16),
        scratch_shapes=[pltpu.VMEM((n + 2 * CONV_HALO, ch), F32)],
        compiler_params=_cparams(1),
        name="conformer_conv",
    )(u, jnp.broadcast_to(conv_w[:, None, :], (CONV_K, SUBLANE, ch)),
      conv_b.reshape(1, ch), ln_g.reshape(1, ch), ln_b.reshape(1, ch))


def _seq_dft_kernel(x_ref, c_ref, s_ref, o_ref, *, norm):
    tr = o_ref.shape[0]
    nf = o_ref.shape[1]
    r0 = pl.multiple_of(pl.program_id(1) * tr, tr)
    acc = _dot(c_ref[pl.ds(r0, tr), :], x_ref[:, :nf]) - _dot(s_ref[pl.ds(r0, tr), :], x_ref[:, nf:])
    o_ref[...] = (acc * norm).astype(BF16)


def _seq_dft(xf, n_batch, n, cmat, smat):
    nf = xf.shape[1] // 2
    tr = min(512, n)
    per = n // tr
    norm = 1.0 / math.sqrt(n * FNET_CH)
    return pl.pallas_call(
        functools.partial(_seq_dft_kernel, norm=norm),
        grid=(n_batch, per),
        in_specs=[pl.BlockSpec((n, 2 * nf), lambda b, j: (b, 0)),
                  _resident(cmat.shape), _resident(smat.shape)],
        out_specs=pl.BlockSpec((tr, nf), lambda b, j: (b * per + j, 0)),
        out_shape=jax.ShapeDtypeStruct((n_batch * n, nf), BF16),
        compiler_params=_cparams(2),
        name="seq_dft",
    )(xf, cmat, smat)


def _rope_cos_sin(n, rot_dim):
    nf = rot_dim // 4
    t = jnp.arange(n)
    row = (t // GRID_W).astype(F32)
    col = (t % GRID_W).astype(F32)
    inv = ROPE_THETA ** (-jnp.arange(nf, dtype=F32) / nf)
    ang = jnp.concatenate([row[:, None] * inv, col[:, None] * inv], axis=-1)
    cos, sin = jnp.cos(ang), jnp.sin(ang)
    lanes = np.arange(rot_dim)
    half = (lanes % (2 * nf)) // nf
    src = (lanes // (2 * nf)) * nf + lanes % nf
    sign = np.where(half == 0, -1.0, 1.0).astype(np.float32)
    partner = np.where(half == 0, lanes + nf, lanes - nf)
    return cos[:, src], sin[:, src] * sign, partner


def _mla_tables(n, tm_ctx):
    cos, sin, partner = _rope_cos_sin(n, MLA_ROPE)
    ones = jnp.ones((n, MLA_NOPE), F32)
    zer = jnp.zeros((n, MLA_NOPE), F32)
    pad = jnp.zeros((n, LANE - MLA_NOPE - MLA_ROPE), F32)
    qa = jnp.concatenate([ones, cos, pad], axis=1) * MLA_SCALE
    qb = jnp.concatenate([zer, sin, pad], axis=1) * MLA_SCALE
    ka = jnp.concatenate([zer, cos, pad], axis=1)
    kb = jnp.concatenate([zer, sin, pad], axis=1)
    lane = np.arange(LANE)
    c_qa = np.where(lane < MLA_NOPE + MLA_ROPE, MLA_SCALE, 0.0).astype(np.float32)
    c_ka = np.where((lane >= MLA_NOPE) & (lane < MLA_NOPE + MLA_ROPE), 1.0, 0.0).astype(np.float32)
    ctx = [jnp.broadcast_to(jnp.asarray(c), (tm_ctx, LANE))
           for c in (c_qa, np.zeros(LANE, np.float32), c_ka, np.zeros(LANE, np.float32))]
    lat = jnp.stack([qa, qb, ka, kb])
    return lat, jnp.stack(ctx), partner


def _gqa_tables(n, tm_ctx, g_qn, g_kn):
    cos, sin, partner = _rope_cos_sin(n, GQA_HD)
    two = lambda a: jnp.concatenate([a, a], axis=-1)
    qa = two(cos * g_qn) * GQA_SCALE
    qb = two(sin * g_qn[partner]) * GQA_SCALE
    ka = two(cos * g_kn)
    kb = two(sin * g_kn[partner])
    lat = jnp.stack([qa, qb, ka, kb])
    zero = jnp.zeros((tm_ctx, LANE), F32)
    ctx = jnp.stack([jnp.broadcast_to(two(g_qn) * GQA_SCALE, (tm_ctx, LANE)), zero,
                     jnp.broadcast_to(two(g_kn), (tm_ctx, LANE)), zero])
    return lat, ctx, partner


def _even_weights(w_in, w_uq, w_ukv, partner):
    d = w_in.shape[0]
    a_cols = 2 * CONV_CH
    kpe0 = a_cols + Q_LORA + KV_LORA
    rope_lo = MLA_NOPE
    zpad = lambda n: jnp.zeros((d, n), F32)
    kpe = w_in[:, kpe0:kpe0 + MLA_ROPE]
    kpe_slab = jnp.concatenate([zpad(rope_lo), kpe, zpad(LANE - rope_lo - MLA_ROPE)], axis=1)
    kpe_sw_slab = jnp.concatenate([zpad(rope_lo), kpe[:, partner],
                                   zpad(LANE - rope_lo - MLA_ROPE)], axis=1)
    win = jnp.concatenate([w_in[:, :kpe0], kpe_slab, kpe_sw_slab], axis=1).astype(BF16)

    dk = MLA_NOPE + MLA_ROPE
    wq3 = w_uq.reshape(Q_LORA, MLA_HEADS, dk)
    qz = jnp.zeros((Q_LORA, MLA_HEADS, LANE - dk), F32)
    wq_pad = jnp.concatenate([wq3, qz], axis=2)
    wq_sw = jnp.concatenate([jnp.zeros((Q_LORA, MLA_HEADS, MLA_NOPE), F32),
                             wq3[:, :, MLA_NOPE:][:, :, partner], qz], axis=2)
    wq = jnp.concatenate([wq_pad.reshape(Q_LORA, -1), wq_sw.reshape(Q_LORA, -1)], axis=1).astype(BF16)

    wkv3 = w_ukv.reshape(KV_LORA, MLA_HEADS, MLA_NOPE + MLA_V)
    kz = jnp.zeros((KV_LORA, MLA_HEADS, LANE - MLA_NOPE), F32)
    wk_pad = jnp.concatenate([wkv3[:, :, :MLA_NOPE], kz], axis=2)
    wv = wkv3[:, :, MLA_NOPE:]
    vz = jnp.zeros_like(wv)
    even = (np.arange(MLA_HEADS) % 2 == 0)[None, :, None]
    wv_pad = jnp.concatenate([jnp.where(even, wv, vz), jnp.where(even, vz, wv)], axis=2)
    wkv = jnp.concatenate([wk_pad.reshape(KV_LORA, -1), wv_pad.reshape(KV_LORA, -1)],
                          axis=1).astype(BF16)
    return win, wq, wkv


def _odd_weights(w_in, partner):
    d = w_in.shape[0]
    nq = GQA_HEADS * GQA_HD
    nkv = GQA_KV_HEADS * GQA_HD
    wq = w_in[:, :nq].reshape(d, GQA_HEADS, GQA_HD)
    wk = w_in[:, nq:nq + nkv].reshape(d, GQA_KV_HEADS, GQA_HD)
    wv = w_in[:, nq + nkv:nq + 2 * nkv].reshape(d, GQA_KV_HEADS, GQA_HD)
    wf = w_in[:, nq + 2 * nkv:]
    dup = lambda a: jnp.concatenate([a, a], axis=2).reshape(d, -1)
    vz = jnp.zeros_like(wv)
    v4 = jnp.stack([jnp.concatenate([wv, vz], axis=2), jnp.concatenate([vz, wv], axis=2)],
                   axis=2).reshape(d, -1)
    return jnp.concatenate([wq.reshape(d, -1), wq[:, :, partner].reshape(d, -1),
                            dup(wk), dup(wk[:, :, partner]), v4, wf], axis=1).astype(BF16)


def _head_mean_matrix():
    i = np.arange(MXU_DIM)
    same = (i[:, None] // GQA_HD) == (i[None, :] // GQA_HD)
    return jnp.asarray(np.where(same, 1.0 / GQA_HD, 0.0), BF16)


def _channel_dft_matrix():
    i = np.arange(MXU_DIM)
    ang = 2.0 * np.pi * ((i[:, None] % FNET_CH) * (i[None, :] % FNET_CH) % FNET_CH) / FNET_CH
    same = (i[:, None] // FNET_CH) == (i[None, :] // FNET_CH)
    c = np.where(same, np.cos(ang), 0.0)
    s = np.where(same, np.sin(ang), 0.0)
    return jnp.asarray(np.concatenate([c, s], axis=1), BF16)


def _seq_dft_matrices(n):
    i = jnp.arange(n, dtype=jnp.int32)
    ang = ((i[:, None] * i[None, :]) % n).astype(F32) * (2.0 * math.pi / n)
    return jnp.cos(ang).astype(BF16), jnp.sin(ang).astype(BF16)


def kernel(x, c, ctx, c_ctx, w_mod, b_mod, g_pre, g_post, ffn_w_gate, ffn_w_up, ffn_w_down,
           ev_w_in, ev_conv_w, ev_conv_b, ev_ln_g, ev_ln_b, ev_g_cq, ev_w_uq, ev_g_ckv,
           ev_w_ukv, ev_w_out, od_w_in, od_g_qn, od_g_kn, od_w_out):
    n_b, n_l, d = x.shape
    n_c = ctx.shape[1]
    depth = w_mod.shape[0]
    nd = w_mod.shape[2]

    bp = -(-(n_b + 1) // SUBLANE) * SUBLANE
    c_all = jnp.concatenate([c, c_ctx[None, :], jnp.zeros((bp - n_b - 1, d), F32)], axis=0)
    mod_all = _modulation(c_all, w_mod, b_mod).reshape(depth, bp, 1, nd)

    xl = x.reshape(n_b * n_l, d)
    xc = ctx.reshape(n_b * n_c, d)
    ctx_rows = n_b * n_c
    tm_c = _row_tile(ctx_rows, ctx_rows)

    mla_lat, mla_ctx, mla_partner = _mla_tables(n_l, tm_c)
    gmat = _head_mean_matrix()
    dft_ch = _channel_dft_matrix()
    cmat_l, smat_l = _seq_dft_matrices(n_l)
    cmat_c, smat_c = _seq_dft_matrices(n_c)

    ffn_w = (ffn_w_gate.astype(BF16), ffn_w_up.astype(BF16), ffn_w_down.astype(BF16))

    for i in range(depth):
        need_ctx = i < depth - 1
        mod = mod_all[i]
        j = i // 2
        xl = _ffn(xl, mod, 0, g_pre[i, 0], g_post[i, 0], ffn_w, (i, 0), rows_per_batch=n_l)
        xc = _ffn(xc, mod, 0, g_pre[i, 0], g_post[i, 0], ffn_w, (i, 0), rows_per_batch=ctx_rows,
                  fixed_row=n_b)
        if i % 2 == 0:
            win, wq, wkv = _even_weights(ev_w_in[j], ev_w_uq[j], ev_w_ukv[j], mla_partner)
            proj = (g_pre[i, 1], win, ev_g_cq[j], wq, ev_g_ckv[j], wkv)
            u_l, q_l, k_l, v_l = _even_in(xl, mod, *proj, mla_lat, rows_per_batch=n_l)
            u_c, q_c, k_c, v_c = _even_in(xc, mod, *proj, mla_ctx, rows_per_batch=ctx_rows,
                                          fixed_row=n_b)
            conv = (ev_conv_w[j], ev_conv_b[j], ev_ln_g[j], ev_ln_b[j])
            ob_l = _attention(q_l, [(k_c, v_c, n_c), (k_l, v_l, n_l)], n_b, n_l, _MLA_MAPS)
            oa_l = _conformer_conv(u_l, n_b, n_l, *conv)
            if need_ctx:
                ob_c = _attention(q_c, [(k_c, v_c, n_c)], n_b, n_c, _MLA_MAPS)
                oa_c = _conformer_conv(u_c, n_b, n_c, *conv)
            w_out = ev_w_out[j].astype(BF16)
        else:
            gqa_lat, gqa_ctx, gqa_partner = _gqa_tables(n_l, tm_c, od_g_qn[j], od_g_kn[j])
            win = _odd_weights(od_w_in[j], gqa_partner)
            proj = (g_pre[i, 1], win, gmat, dft_ch)
            q_l, k_l, v_l, f_l = _odd_in(xl, mod, *proj, gqa_lat, rows_per_batch=n_l)
            q_c, k_c, v_c, f_c = _odd_in(xc, mod, *proj, gqa_ctx, rows_per_batch=ctx_rows,
                                         fixed_row=n_b)
            oa_l = _attention(q_l, [(k_c, v_c, n_c), (k_l, v_l, n_l)], n_b, n_l, _GQA_MAPS)
            ob_l = _seq_dft(f_l, n_b, n_l, cmat_l, smat_l)
            if need_ctx:
                oa_c = _attention(q_c, [(k_c, v_c, n_c)], n_b, n_c, _GQA_MAPS)
                ob_c = _seq_dft(f_c, n_b, n_c, cmat_c, smat_c)
            w_out = od_w_out[j].astype(BF16)
        xl = _ffn(xl, mod, 2, g_pre[i, 2], g_post[i, 2], ffn_w, (i, 1), rows_per_batch=n_l,
                  mix=(oa_l, ob_l, w_out, g_post[i, 1]))
        if need_ctx:
            xc = _ffn(xc, mod, 2, g_pre[i, 2], g_post[i, 2], ffn_w, (i, 1), rows_per_batch=ctx_rows,
                      fixed_row=n_b, mix=(oa_c, ob_c, w_out, g_post[i, 1]))
    return xl.reshape(n_b, n_l, d)
```

```python
import functools
import math

import numpy as np
import jax
import jax.numpy as jnp
from jax import lax
from jax.experimental import pallas as pl
from jax.experimental.pallas import tpu as pltpu

F32 = jnp.float32
BF16 = jnp.bfloat16

N_MOD = 9
FFN_RES = 0.5
EPS = 1e-6
ROPE_THETA = 10000.0
GRID_W = 64
CONV_CH = 512
CONV_K = 31
MLA_HEADS = 8
MLA_NOPE = 64
MLA_ROPE = 32
MLA_V = 64
Q_LORA = 384
KV_LORA = 256
LOG2E = math.log2(math.e)
MLA_SCALE = (MLA_NOPE + MLA_ROPE) ** -0.5 * LOG2E
GQA_HEADS = 8
GQA_KV_HEADS = 2
GQA_GROUP = GQA_HEADS // GQA_KV_HEADS
GQA_HD = 64
GQA_SCALE = GQA_HD ** -0.5 * LOG2E
FNET_GROUPS = 4
FNET_CH = 128
HEADS = 8
HEAD_V = 64

LANE = 128
SUBLANE = 8
MXU_DIM = 256
VMEM_LIMIT = 56 * 1024 * 1024
FF_CHUNK = 256
FFN_ROWS = 1024
FFN_SUB = 512
CONV_HALO = 16
CONV_ROWS = 64
ATTN_SUB = 512


def _cparams(n_axes):
    return pltpu.CompilerParams(dimension_semantics=("arbitrary",) * n_axes,
                                vmem_limit_bytes=VMEM_LIMIT)


def _resident(shape):
    zeros = (0,) * len(shape)
    return pl.BlockSpec(shape, lambda *_: zeros, pipeline_mode=pl.Buffered(1))


def _rms(x):
    return x * lax.rsqrt(jnp.mean(x * x, axis=-1, keepdims=True) + EPS)


def _sigmoid(x):
    return 1.0 / (1.0 + jnp.exp(-x))


def _dot(a, b):
    return jnp.dot(a, b, preferred_element_type=F32)


def _dot_t(a, b):
    return lax.dot_general(a, b, (((1,), (1,)), ((), ())), preferred_element_type=F32)


def _mod_kernel(c_ref, w_ref, b_ref, o_ref):
    c = c_ref[...]
    sc = (c * _sigmoid(c)).astype(BF16)
    o_ref[...] = _dot(sc, w_ref[...].astype(BF16)) + b_ref[...]


def _modulation(c_all, w_mod, b_mod):
    depth, d, nd = w_mod.shape
    bp = c_all.shape[0]
    tn = 1024
    return pl.pallas_call(
        _mod_kernel,
        grid=(depth, nd // tn),
        in_specs=[pl.BlockSpec((bp, d), lambda i, j: (0, 0)),
                  pl.BlockSpec((None, d, tn), lambda i, j: (i, 0, j)),
                  pl.BlockSpec((None, 1, tn), lambda i, j: (i, 0, j))],
        out_specs=pl.BlockSpec((None, bp, tn), lambda i, j: (i, 0, j)),
        out_shape=jax.ShapeDtypeStruct((depth, bp, nd), F32),
        compiler_params=_cparams(2),
        name="modulation",
    )(c_all, w_mod, b_mod.reshape(depth, 1, nd))


def _ffn_body(x, mod_ref, k, gpre_ref, gpost_ref, wg_ref, wu_ref, wd_ref, h_ref, u_ref, rows):
    d = x.shape[-1]
    shift = mod_ref[:, (3 * k) * d:(3 * k + 1) * d]
    scale = mod_ref[:, (3 * k + 1) * d:(3 * k + 2) * d]
    gate = mod_ref[:, (3 * k + 2) * d:(3 * k + 3) * d]
    h = _rms(x) * gpre_ref[...] * (1.0 + scale) + shift
    h_ref[rows, :] = h.astype(BF16)
    ch = FF_CHUNK
    for c in range(wg_ref.shape[1] // ch):
        a = _dot(h_ref[rows, :], wg_ref[:, c * ch:(c + 1) * ch])
        b = _dot(h_ref[rows, :], wu_ref[:, c * ch:(c + 1) * ch])
        u_ref[rows, c * ch:(c + 1) * ch] = (a * _sigmoid(a) * b).astype(BF16)
    y = _dot(u_ref[rows, :], wd_ref[...])
    return x + (FFN_RES * gate) * (_rms(y) * gpost_ref[...])


def _sub_tiles(n):
    sub = min(n, FFN_SUB)
    return [slice(r, r + sub) for r in range(0, n, sub)]


def _ffn_kernel(x_ref, mod_ref, gpre_ref, gpost_ref, wg_ref, wu_ref, wd_ref, o_ref, h_ref, u_ref,
                *, k):
    for rows in _sub_tiles(x_ref.shape[0]):
        o_ref[rows, :] = _ffn_body(x_ref[rows, :], mod_ref, k, gpre_ref, gpost_ref, wg_ref, wu_ref,
                                   wd_ref, h_ref, u_ref, rows)


def _mix_ffn_kernel(x_ref, oa_ref, ob_ref, wo_ref, gmix_ref, mod_ref, gpre_ref, gpost_ref,
                    wg_ref, wu_ref, wd_ref, o_ref, h_ref, u_ref):
    d = x_ref.shape[-1]
    half = oa_ref.shape[-1]
    gate = mod_ref[:, 5 * d:6 * d]
    for rows in _sub_tiles(x_ref.shape[0]):
        y = _dot(oa_ref[rows, :], wo_ref[:half, :]) + _dot(ob_ref[rows, :], wo_ref[half:, :])
        x1 = x_ref[rows, :] + gate * (_rms(y) * gmix_ref[...])
        o_ref[rows, :] = _ffn_body(x1, mod_ref, 2, gpre_ref, gpost_ref, wg_ref, wu_ref, wd_ref,
                                   h_ref, u_ref, rows)


def _row_tile(rows_per_batch, total_rows, want=512):
    tm = min(want, rows_per_batch)
    assert rows_per_batch % tm == 0 and total_rows % tm == 0
    return tm


def _mod_spec(nd, rows_per_batch, tm, fixed_row):
    if fixed_row is not None:
        return pl.BlockSpec((None, 1, nd), lambda t: (fixed_row, 0, 0))
    per = rows_per_batch // tm
    return pl.BlockSpec((None, 1, nd), lambda t: (t // per, 0, 0))


def _layer_weight(w, i, j):
    return pl.BlockSpec((None, None) + w.shape[2:], lambda *_: (i, j, 0, 0),
                        pipeline_mode=pl.Buffered(1))


def _ffn(x, mod, k, g_pre, g_post, ffn_w, ij, rows_per_batch, fixed_row=None, mix=None):
    r, d = x.shape
    nd = mod.shape[-1]
    wg, wu, wd = ffn_w
    dff = wd.shape[2]
    tm = _row_tile(rows_per_batch, r, want=FFN_ROWS)
    row = lambda t: (t, 0)
    vec = pl.BlockSpec((1, d), lambda t: (0, 0))
    tail_specs = [_mod_spec(nd, rows_per_batch, tm, fixed_row), vec, vec,
                  _layer_weight(wg, *ij), _layer_weight(wu, *ij), _layer_weight(wd, *ij)]
    tail_args = [mod, g_pre.reshape(1, d), g_post.reshape(1, d), wg, wu, wd]
    scratch = [pltpu.VMEM((tm, d), BF16), pltpu.VMEM((tm, dff), BF16)]
    if mix is None:
        kern = functools.partial(_ffn_kernel, k=k)
        in_specs = [pl.BlockSpec((tm, d), row)] + tail_specs
        args = [x] + tail_args
        name = "ffn"
    else:
        oa, ob, wo, g_mix = mix
        half = oa.shape[-1]
        kern = _mix_ffn_kernel
        in_specs = [pl.BlockSpec((tm, d), row), pl.BlockSpec((tm, half), row),
                    pl.BlockSpec((tm, half), row), _resident(wo.shape), vec] + tail_specs
        args = [x, oa, ob, wo, g_mix.reshape(1, d)] + tail_args
        name = "mix_ffn"
    return pl.pallas_call(
        kern,
        grid=(r // tm,),
        in_specs=in_specs,
        out_specs=pl.BlockSpec((tm, d), row),
        out_shape=jax.ShapeDtypeStruct((r, d), F32),
        scratch_shapes=scratch,
        compiler_params=_cparams(1),
        name=name,
    )(*args)


def _even_in_kernel(x_ref, mod_ref, gpre_ref, win_ref, gcq_ref, wq_ref, gckv_ref, wkv_ref, tab_ref,
                    u_ref, q_ref, k_ref, v_ref):
    x = x_ref[...]
    d = x.shape[-1]
    shift = mod_ref[:, 3 * d:4 * d]
    scale = mod_ref[:, 4 * d:5 * d]
    h = (_rms(x) * gpre_ref[...] * (1.0 + scale) + shift).astype(BF16)
    p = _dot(h, win_ref[...])
    c0 = CONV_CH
    c1 = 2 * CONV_CH
    c2 = c1 + Q_LORA
    c3 = c2 + KV_LORA
    val = p[:, :c0]
    gat = p[:, c0:c1]
    u_ref[...] = val * _sigmoid(gat)
    cqn = (_rms(p[:, c1:c2]) * gcq_ref[...]).astype(BF16)
    ckvn = (_rms(p[:, c2:c3]) * gckv_ref[...]).astype(BF16)
    kpe = p[:, c3:c3 + LANE]
    kpe_sw = p[:, c3 + LANE:c3 + 2 * LANE]
    qa, qb, ka, kb = tab_ref[0], tab_ref[1], tab_ref[2], tab_ref[3]
    kpe_rot = kpe * ka + kpe_sw * kb
    qq = _dot(cqn, wq_ref[...])
    kv = _dot(ckvn, wkv_ref[...])
    hw = MLA_HEADS * LANE
    for hd in range(MLA_HEADS):
        lo, hi = hd * LANE, (hd + 1) * LANE
        q_ref[:, lo:hi] = (qq[:, lo:hi] * qa + qq[:, hw + lo:hw + hi] * qb).astype(BF16)
        k_ref[:, lo:hi] = (kv[:, lo:hi] + kpe_rot).astype(BF16)
    v_ref[...] = kv[:, hw:].astype(BF16)


def _even_in(x, mod, g_pre, win, gcq, wq, gckv, wkv, tabs, rows_per_batch, fixed_row=None):
    r, d = x.shape
    nd = mod.shape[-1]
    tm = _row_tile(rows_per_batch, r)
    per = rows_per_batch // tm
    row = lambda t: (t, 0)
    hw = MLA_HEADS * LANE
    if fixed_row is None:
        tab_spec = pl.BlockSpec((4, tm, LANE), lambda t: (0, t % per, 0))
    else:
        tab_spec = pl.BlockSpec((4, tm, LANE), lambda t: (0, 0, 0))
    return pl.pallas_call(
        _even_in_kernel,
        grid=(r // tm,),
        in_specs=[pl.BlockSpec((tm, d), row),
                  _mod_spec(nd, rows_per_batch, tm, fixed_row),
                  pl.BlockSpec((1, d), lambda t: (0, 0)),
                  _resident(win.shape),
                  pl.BlockSpec((1, Q_LORA), lambda t: (0, 0)),
                  _resident(wq.shape),
                  pl.BlockSpec((1, KV_LORA), lambda t: (0, 0)),
                  _resident(wkv.shape),
                  tab_spec],
        out_specs=[pl.BlockSpec((tm, CONV_CH), row), pl.BlockSpec((tm, hw), row),
                   pl.BlockSpec((tm, hw), row), pl.BlockSpec((tm, hw), row)],
        out_shape=[jax.ShapeDtypeStruct((r, CONV_CH), F32), jax.ShapeDtypeStruct((r, hw), BF16),
                   jax.ShapeDtypeStruct((r, hw), BF16), jax.ShapeDtypeStruct((r, hw), BF16)],
        compiler_params=_cparams(1),
        name="even_in",
    )(x, mod, g_pre.reshape(1, d), win, gcq.reshape(1, -1), wq, gckv.reshape(1, -1), wkv, tabs)


def _head_mean_sq(x, g_ref):
    sq = x * x
    hi = sq.astype(BF16)
    lo = (sq - hi.astype(F32)).astype(BF16)
    return _dot(hi, g_ref[...]) + _dot(lo, g_ref[...])


def _odd_in_kernel(x_ref, mod_ref, gpre_ref, win_ref, gmat_ref, dft_ref, tab_ref,
                   q_ref, k_ref, v_ref, f_ref):
    x = x_ref[...]
    d = x.shape[-1]
    shift = mod_ref[:, 3 * d:4 * d]
    scale = mod_ref[:, 4 * d:5 * d]
    h = (_rms(x) * gpre_ref[...] * (1.0 + scale) + shift).astype(BF16)
    p = _dot(h, win_ref[...])
    nq = GQA_HEADS * GQA_HD
    nk = GQA_KV_HEADS * LANE
    o_qs = nq
    o_k = 2 * nq
    o_ks = o_k + nk
    o_v = o_ks + nk
    o_f = o_v + 2 * nk
    qa, qb, ka, kb = tab_ref[0], tab_ref[1], tab_ref[2], tab_ref[3]
    for c in range(nq // MXU_DIM):
        lo = c * MXU_DIM
        qc = p[:, lo:lo + MXU_DIM]
        qs = p[:, o_qs + lo:o_qs + lo + MXU_DIM]
        r = lax.rsqrt(_head_mean_sq(qc, gmat_ref) + EPS)
        for s in range(MXU_DIM // LANE):
            a, b = s * LANE, (s + 1) * LANE
            q_ref[:, lo + a:lo + b] = (r[:, a:b] * (qc[:, a:b] * qa + qs[:, a:b] * qb)).astype(BF16)
    kc = p[:, o_k:o_k + nk]
    ks = p[:, o_ks:o_ks + nk]
    r = lax.rsqrt(_head_mean_sq(kc, gmat_ref) + EPS)
    for s in range(nk // LANE):
        a, b = s * LANE, (s + 1) * LANE
        k_ref[:, a:b] = (r[:, a:b] * (kc[:, a:b] * ka + ks[:, a:b] * kb)).astype(BF16)
    v_ref[...] = p[:, o_v:o_f].astype(BF16)
    nf = FNET_GROUPS * FNET_CH
    for c in range(nf // MXU_DIM):
        lo = c * MXU_DIM
        fx = p[:, o_f + lo:o_f + lo + MXU_DIM].astype(BF16)
        cs = _dot(fx, dft_ref[...])
        f_ref[:, lo:lo + MXU_DIM] = cs[:, :MXU_DIM].astype(BF16)
        f_ref[:, nf + lo:nf + lo + MXU_DIM] = cs[:, MXU_DIM:].astype(BF16)


def _odd_in(x, mod, g_pre, win, gmat, dft, tabs, rows_per_batch, fixed_row=None):
    r, d = x.shape
    nd = mod.shape[-1]
    tm = _row_tile(rows_per_batch, r)
    per = rows_per_batch // tm
    row = lambda t: (t, 0)
    nq = GQA_HEADS * GQA_HD
    nk = GQA_KV_HEADS * LANE
    nf = FNET_GROUPS * FNET_CH
    if fixed_row is None:
        tab_spec = pl.BlockSpec((4, tm, LANE), lambda t: (0, t % per, 0))
    else:
        tab_spec = pl.BlockSpec((4, tm, LANE), lambda t: (0, 0, 0))
    return pl.pallas_call(
        _odd_in_kernel,
        grid=(r // tm,),
        in_specs=[pl.BlockSpec((tm, d), row),
                  _mod_spec(nd, rows_per_batch, tm, fixed_row),
                  pl.BlockSpec((1, d), lambda t: (0, 0)),
                  _resident(win.shape),
                  _resident(gmat.shape),
                  _resident(dft.shape),
                  tab_spec],
        out_specs=[pl.BlockSpec((tm, nq), row), pl.BlockSpec((tm, nk), row),
                   pl.BlockSpec((tm, 2 * nk), row), pl.BlockSpec((tm, 2 * nf), row)],
        out_shape=[jax.ShapeDtypeStruct((r, nq), BF16), jax.ShapeDtypeStruct((r, nk), BF16),
                   jax.ShapeDtypeStruct((r, 2 * nk), BF16), jax.ShapeDtypeStruct((r, 2 * nf), BF16)],
        compiler_params=_cparams(1),
        name="odd_in",
    )(x, mod, g_pre.reshape(1, d), win, gmat, dft, tabs)


def _attn_kernel(*refs, n_src, q_slab, q_half, k_slab, v_slab):
    q_ref = refs[0]
    k_refs = refs[1:1 + n_src]
    v_refs = refs[1 + n_src:1 + 2 * n_src]
    o_ref = refs[1 + 2 * n_src]
    tq = q_ref.shape[0]
    sub = min(tq, ATTN_SUB)
    lane = lax.broadcasted_iota(jnp.int32, (sub, LANE), 1)
    for r0 in range(0, tq, sub):
        acc = [None] * (HEADS // 2)
        for hd in range(HEADS):
            q = q_ref[r0:r0 + sub, q_slab[hd] * LANE:(q_slab[hd] + 1) * LANE]
            if q_half[hd] is not None:
                keep = (lane < HEAD_V) if q_half[hd] == 0 else (lane >= HEAD_V)
                q = jnp.where(keep, q, jnp.zeros_like(q))
            ks, vs = k_slab[hd], v_slab[hd]
            s = [_dot_t(q, kr[:, ks * LANE:(ks + 1) * LANE]) for kr in k_refs]
            m = s[0].max(axis=-1, keepdims=True)
            for si in s[1:]:
                m = jnp.maximum(m, si.max(axis=-1, keepdims=True))
            p = [jnp.exp2(si - m) for si in s]
            den = p[0].sum(axis=-1, keepdims=True)
            for pi in p[1:]:
                den = den + pi.sum(axis=-1, keepdims=True)
            o = _dot(p[0].astype(BF16), v_refs[0][:, vs * LANE:(vs + 1) * LANE])
            for pi, vr in zip(p[1:], v_refs[1:]):
                o = o + _dot(pi.astype(BF16), vr[:, vs * LANE:(vs + 1) * LANE])
            o = o / den
            acc[hd // 2] = o if acc[hd // 2] is None else acc[hd // 2] + o
        for j in range(HEADS // 2):
            o_ref[r0:r0 + sub, j * LANE:(j + 1) * LANE] = acc[j].astype(BF16)


def _attention(q, kvs, n_batch, lq, head_maps, tq=512):
    tq = min(tq, lq)
    per = lq // tq
    n_src = len(kvs)
    wq = q.shape[1]
    in_specs = [pl.BlockSpec((tq, wq), lambda b, i: (b * per + i, 0))]
    for k, _, lk in kvs:
        in_specs.append(pl.BlockSpec((lk, k.shape[1]), lambda b, i: (b, 0)))
    for _, v, lk in kvs:
        in_specs.append(pl.BlockSpec((lk, v.shape[1]), lambda b, i: (b, 0)))
    kern = functools.partial(_attn_kernel, n_src=n_src, **head_maps)
    return pl.pallas_call(
        kern,
        grid=(n_batch, per),
        in_specs=in_specs,
        out_specs=pl.BlockSpec((tq, HEADS * HEAD_V), lambda b, i: (b * per + i, 0)),
        out_shape=jax.ShapeDtypeStruct((n_batch * lq, HEADS * HEAD_V), BF16),
        compiler_params=_cparams(2),
        name="attention",
    )(q, *[k for k, _, _ in kvs], *[v for _, v, _ in kvs])


_MLA_MAPS = dict(q_slab=tuple(range(8)), q_half=(None,) * 8,
                 k_slab=tuple(range(8)), v_slab=tuple(range(8)))
_GQA_MAPS = dict(q_slab=tuple(h // 2 for h in range(8)), q_half=tuple(h % 2 for h in range(8)),
                 k_slab=tuple(h // GQA_GROUP for h in range(8)),
                 v_slab=tuple(2 * (h // GQA_GROUP) + h % 2 for h in range(8)))


def _conv_kernel(u_ref, w_ref, b_ref, g_ref, beta_ref, o_ref, pad_ref):
    n, ch = u_ref.shape
    zeros = jnp.zeros((CONV_HALO, ch), F32)
    pad_ref[0:CONV_HALO, :] = zeros
    pad_ref[CONV_HALO + n:2 * CONV_HALO + n, :] = zeros
    pad_ref[CONV_HALO:CONV_HALO + n, :] = u_ref[...]
    first = CONV_HALO - CONV_K // 2
    span = CONV_ROWS + 2 * CONV_HALO

    def step(i, carry):
        r0 = pl.multiple_of(i * CONV_ROWS, CONV_ROWS)
        cols = []
        for c0 in range(0, ch, LANE):
            win = pad_ref[pl.ds(r0, span), c0:c0 + LANE]
            acc = None
            for s in range(SUBLANE):
                sh = win if s == 0 else pltpu.roll(win, span - s, axis=0)
                sh = sh.reshape(span // SUBLANE, SUBLANE, LANE)
                for k in range(CONV_K):
                    a, sk = divmod(first + k, SUBLANE)
                    if sk == s:
                        term = sh[a:a + CONV_ROWS // SUBLANE] * w_ref[k, :, c0:c0 + LANE][None]
                        acc = term if acc is None else acc + term
            cols.append(acc.reshape(CONV_ROWS, LANE) + b_ref[:, c0:c0 + LANE])
        v = jnp.concatenate(cols, axis=1)
        mu = jnp.mean(v, axis=-1, keepdims=True)
        cen = v - mu
        var = jnp.mean(cen * cen, axis=-1, keepdims=True)
        y = cen * lax.rsqrt(var + EPS) * g_ref[...] + beta_ref[...]
        o_ref[pl.ds(r0, CONV_ROWS), :] = (y * _sigmoid(y)).astype(BF16)
        return carry

    lax.fori_loop(0, n // CONV_ROWS, step, 0)


def _conformer_conv(u, n_batch, n, conv_w, conv_b, ln_g, ln_b):
    ch = u.shape[-1]
    vec = pl.BlockSpec((1, ch), lambda b: (0, 0))
    return pl.pallas_call(
        _conv_kernel,
        grid=(n_batch,),
        in_specs=[pl.BlockSpec((n, ch), lambda b: (b, 0)),
                  pl.BlockSpec((CONV_K, SUBLANE, ch), lambda b: (0, 0, 0)), vec, vec, vec],
        out_specs=pl.BlockSpec((n, ch), lambda b: (b, 0)),
        out_shape=jax.ShapeDtypeStruct((n_batch * n, ch), BF16),
        scratch_shapes=[pltpu.VMEM((n + 2 * CONV_HALO, ch), F32)],
        compiler_params=_cparams(1),
        name="conformer_conv",
    )(u, jnp.broadcast_to(conv_w[:, None, :], (CONV_K, SUBLANE, ch)),
      conv_b.reshape(1, ch), ln_g.reshape(1, ch), ln_b.reshape(1, ch))


def _seq_dft_kernel(x_ref, c_ref, s_ref, o_ref, *, norm):
    tr = o_ref.shape[0]
    nf = o_ref.shape[1]
    r0 = pl.multiple_of(pl.program_id(1) * tr, tr)
    acc = _dot(c_ref[pl.ds(r0, tr), :], x_ref[:, :nf]) - _dot(s_ref[pl.ds(r0, tr), :], x_ref[:, nf:])
    o_ref[...] = (acc * norm).astype(BF16)


def _seq_dft(xf, n_batch, n, cmat, smat):
    nf = xf.shape[1] // 2
    tr = min(512, n)
    per = n // tr
    norm = 1.0 / math.sqrt(n * FNET_CH)
    return pl.pallas_call(
        functools.partial(_seq_dft_kernel, norm=norm),
        grid=(n_batch, per),
        in_specs=[pl.BlockSpec((n, 2 * nf), lambda b, j: (b, 0)),
                  _resident(cmat.shape), _resident(smat.shape)],
        out_specs=pl.BlockSpec((tr, nf), lambda b, j: (b * per + j, 0)),
        out_shape=jax.ShapeDtypeStruct((n_batch * n, nf), BF16),
        compiler_params=_cparams(2),
        name="seq_dft",
    )(xf, cmat, smat)


def _rope_cos_sin(n, rot_dim):
    nf = rot_dim // 4
    t = jnp.arange(n)
    row = (t // GRID_W).astype(F32)
    col = (t % GRID_W).astype(F32)
    inv = ROPE_THETA ** (-jnp.arange(nf, dtype=F32) / nf)
    ang = jnp.concatenate([row[:, None] * inv, col[:, None] * inv], axis=-1)
    cos, sin = jnp.cos(ang), jnp.sin(ang)
    lanes = np.arange(rot_dim)
    half = (lanes % (2 * nf)) // nf
    src = (lanes // (2 * nf)) * nf + lanes % nf
    sign = np.where(half == 0, -1.0, 1.0).astype(np.float32)
    partner = np.where(half == 0, lanes + nf, lanes - nf)
    return cos[:, src], sin[:, src] * sign, partner


def _mla_tables(n, tm_ctx):
    cos, sin, partner = _rope_cos_sin(n, MLA_ROPE)
    ones = jnp.ones((n, MLA_NOPE), F32)
    zer = jnp.zeros((n, MLA_NOPE), F32)
    pad = jnp.zeros((n, LANE - MLA_NOPE - MLA_ROPE), F32)
    qa = jnp.concatenate([ones, cos, pad], axis=1) * MLA_SCALE
    qb = jnp.concatenate([zer, sin, pad], axis=1) * MLA_SCALE
    ka = jnp.concatenate([zer, cos, pad], axis=1)
    kb = jnp.concatenate([zer, sin, pad], axis=1)
    lane = np.arange(LANE)
    c_qa = np.where(lane < MLA_NOPE + MLA_ROPE, MLA_SCALE, 0.0).astype(np.float32)
    c_ka = np.where((lane >= MLA_NOPE) & (lane < MLA_NOPE + MLA_ROPE), 1.0, 0.0).astype(np.float32)
    ctx = [jnp.broadcast_to(jnp.asarray(c), (tm_ctx, LANE))
           for c in (c_qa, np.zeros(LANE, np.float32), c_ka, np.zeros(LANE, np.float32))]
    lat = jnp.stack([qa, qb, ka, kb])
    return lat, jnp.stack(ctx), partner


def _gqa_tables(n, tm_ctx, g_qn, g_kn):
    cos, sin, partner = _rope_cos_sin(n, GQA_HD)
    two = lambda a: jnp.concatenate([a, a], axis=-1)
    qa = two(cos * g_qn) * GQA_SCALE
    qb = two(sin * g_qn[partner]) * GQA_SCALE
    ka = two(cos * g_kn)
    kb = two(sin * g_kn[partner])
    lat = jnp.stack([qa, qb, ka, kb])
    zero = jnp.zeros((tm_ctx, LANE), F32)
    ctx = jnp.stack([jnp.broadcast_to(two(g_qn) * GQA_SCALE, (tm_ctx, LANE)), zero,
                     jnp.broadcast_to(two(g_kn), (tm_ctx, LANE)), zero])
    return lat, ctx, partner


def _even_weights(w_in, w_uq, w_ukv, partner):
    d = w_in.shape[0]
    a_cols = 2 * CONV_CH
    kpe0 = a_cols + Q_LORA + KV_LORA
    rope_lo = MLA_NOPE
    zpad = lambda n: jnp.zeros((d, n), F32)
    kpe = w_in[:, kpe0:kpe0 + MLA_ROPE]
    kpe_slab = jnp.concatenate([zpad(rope_lo), kpe, zpad(LANE - rope_lo - MLA_ROPE)], axis=1)
    kpe_sw_slab = jnp.concatenate([zpad(rope_lo), kpe[:, partner],
                                   zpad(LANE - rope_lo - MLA_ROPE)], axis=1)
    win = jnp.concatenate([w_in[:, :kpe0], kpe_slab, kpe_sw_slab], axis=1).astype(BF16)

    dk = MLA_NOPE + MLA_ROPE
    wq3 = w_uq.reshape(Q_LORA, MLA_HEADS, dk)
    qz = jnp.zeros((Q_LORA, MLA_HEADS, LANE - dk), F32)
    wq_pad = jnp.concatenate([wq3, qz], axis=2)
    wq_sw = jnp.concatenate([jnp.zeros((Q_LORA, MLA_HEADS, MLA_NOPE), F32),
                             wq3[:, :, MLA_NOPE:][:, :, partner], qz], axis=2)
    wq = jnp.concatenate([wq_pad.reshape(Q_LORA, -1), wq_sw.reshape(Q_LORA, -1)], axis=1).astype(BF16)

    wkv3 = w_ukv.reshape(KV_LORA, MLA_HEADS, MLA_NOPE + MLA_V)
    kz = jnp.zeros((KV_LORA, MLA_HEADS, LANE - MLA_NOPE), F32)
    wk_pad = jnp.concatenate([wkv3[:, :, :MLA_NOPE], kz], axis=2)
    wv = wkv3[:, :, MLA_NOPE:]
    vz = jnp.zeros_like(wv)
    even = (np.arange(MLA_HEADS) % 2 == 0)[None, :, None]
    wv_pad = jnp.concatenate([jnp.where(even, wv, vz), jnp.where(even, vz, wv)], axis=2)
    wkv = jnp.concatenate([wk_pad.reshape(KV_LORA, -1), wv_pad.reshape(KV_LORA, -1)],
                          axis=1).astype(BF16)
    return win, wq, wkv


def _odd_weights(w_in, partner):
    d = w_in.shape[0]
    nq = GQA_HEADS * GQA_HD
    nkv = GQA_KV_HEADS * GQA_HD
    wq = w_in[:, :nq].reshape(d, GQA_HEADS, GQA_HD)
    wk = w_in[:, nq:nq + nkv].reshape(d, GQA_KV_HEADS, GQA_HD)
    wv = w_in[:, nq + nkv:nq + 2 * nkv].reshape(d, GQA_KV_HEADS, GQA_HD)
    wf = w_in[:, nq + 2 * nkv:]
    dup = lambda a: jnp.concatenate([a, a], axis=2).reshape(d, -1)
    vz = jnp.zeros_like(wv)
    v4 = jnp.stack([jnp.concatenate([wv, vz], axis=2), jnp.concatenate([vz, wv], axis=2)],
                   axis=2).reshape(d, -1)
    return jnp.concatenate([wq.reshape(d, -1), wq[:, :, partner].reshape(d, -1),
                            dup(wk), dup(wk[:, :, partner]), v4, wf], axis=1).astype(BF16)


def _head_mean_matrix():
    i = np.arange(MXU_DIM)
    same = (i[:, None] // GQA_HD) == (i[None, :] // GQA_HD)
    return jnp.asarray(np.where(same, 1.0 / GQA_HD, 0.0), BF16)


def _channel_dft_matrix():
    i = np.arange(MXU_DIM)
    ang = 2.0 * np.pi * ((i[:, None] % FNET_CH) * (i[None, :] % FNET_CH) % FNET_CH) / FNET_CH
    same = (i[:, None] // FNET_CH) == (i[None, :] // FNET_CH)
    c = np.where(same, np.cos(ang), 0.0)
    s = np.where(same, np.sin(ang), 0.0)
    return jnp.asarray(np.concatenate([c, s], axis=1), BF16)


def _seq_dft_matrices(n):
    i = jnp.arange(n, dtype=jnp.int32)
    ang = ((i[:, None] * i[None, :]) % n).astype(F32) * (2.0 * math.pi / n)
    return jnp.cos(ang).astype(BF16), jnp.sin(ang).astype(BF16)


def kernel(x, c, ctx, c_ctx, w_mod, b_mod, g_pre, g_post, ffn_w_gate, ffn_w_up, ffn_w_down,
           ev_w_in, ev_conv_w, ev_conv_b, ev_ln_g, ev_ln_b, ev_g_cq, ev_w_uq, ev_g_ckv,
           ev_w_ukv, ev_w_out, od_w_in, od_g_qn, od_g_kn, od_w_out):
    n_b, n_l, d = x.shape
    n_c = ctx.shape[1]
    depth = w_mod.shape[0]
    nd = w_mod.shape[2]

    bp = -(-(n_b + 1) // SUBLANE) * SUBLANE
    c_all = jnp.concatenate([c, c_ctx[None, :], jnp.zeros((bp - n_b - 1, d), F32)], axis=0)
    mod_all = _modulation(c_all, w_mod, b_mod).reshape(depth, bp, 1, nd)

    xl = x.reshape(n_b * n_l, d)
    xc = ctx.reshape(n_b * n_c, d)
    ctx_rows = n_b * n_c
    tm_c = _row_tile(ctx_rows, ctx_rows)

    mla_lat, mla_ctx, mla_partner = _mla_tables(n_l, tm_c)
    gmat = _head_mean_matrix()
    dft_ch = _channel_dft_matrix()
    cmat_l, smat_l = _seq_dft_matrices(n_l)
    cmat_c, smat_c = _seq_dft_matrices(n_c)

    ffn_w = (ffn_w_gate.astype(BF16), ffn_w_up.astype(BF16), ffn_w_down.astype(BF16))

    for i in range(depth):
        need_ctx = i < depth - 1
        mod = mod_all[i]
        j = i // 2
        xl = _ffn(xl, mod, 0, g_pre[i, 0], g_post[i, 0], ffn_w, (i, 0), rows_per_batch=n_l)
        xc = _ffn(xc, mod, 0, g_pre[i, 0], g_post[i, 0], ffn_w, (i, 0), rows_per_batch=ctx_rows,
                  fixed_row=n_b)
        if i % 2 == 0:
            win, wq, wkv = _even_weights(ev_w_in[j], ev_w_uq[j], ev_w_ukv[j], mla_partner)
            proj = (g_pre[i, 1], win, ev_g_cq[j], wq, ev_g_ckv[j], wkv)
            u_l, q_l, k_l, v_l = _even_in(xl, mod, *proj, mla_lat, rows_per_batch=n_l)
            u_c, q_c, k_c, v_c = _even_in(xc, mod, *proj, mla_ctx, rows_per_batch=ctx_rows,
                                          fixed_row=n_b)
            conv = (ev_conv_w[j], ev_conv_b[j], ev_ln_g[j], ev_ln_b[j])
            ob_l = _attention(q_l, [(k_c, v_c, n_c), (k_l, v_l, n_l)], n_b, n_l, _MLA_MAPS)
            oa_l = _conformer_conv(u_l, n_b, n_l, *conv)
            if need_ctx:
                ob_c = _attention(q_c, [(k_c, v_c, n_c)], n_b, n_c, _MLA_MAPS)
                oa_c = _conformer_conv(u_c, n_b, n_c, *conv)
            w_out = ev_w_out[j].astype(BF16)
        else:
            gqa_lat, gqa_ctx, gqa_partner = _gqa_tables(n_l, tm_c, od_g_qn[j], od_g_kn[j])
            win = _odd_weights(od_w_in[j], gqa_partner)
            proj = (g_pre[i, 1], win, gmat, dft_ch)
            q_l, k_l, v_l, f_l = _odd_in(xl, mod, *proj, gqa_lat, rows_per_batch=n_l)
            q_c, k_c, v_c, f_c = _odd_in(xc, mod, *proj, gqa_ctx, rows_per_batch=ctx_rows,
                                         fixed_row=n_b)
            oa_l = _attention(q_l, [(k_c, v_c, n_c), (k_l, v_l, n_l)], n_b, n_l, _GQA_MAPS)
            ob_l = _seq_dft(f_l, n_b, n_l, cmat_l, smat_l)
            if need_ctx:
                oa_c = _attention(q_c, [(k_c, v_c, n_c)], n_b, n_c, _GQA_MAPS)
                ob_c = _seq_dft(f_c, n_b, n_c, cmat_c, smat_c)
            w_out = od_w_out[j].astype(BF16)
        xl = _ffn(xl, mod, 2, g_pre[i, 2], g_post[i, 2], ffn_w, (i, 1), rows_per_batch=n_l,
                  mix=(oa_l, ob_l, w_out, g_post[i, 1]))
        if need_ctx:
            xc = _ffn(xc, mod, 2, g_pre[i, 2], g_post[i, 2], ffn_w, (i, 1), rows_per_batch=ctx_rows,
                      fixed_row=n_b, mix=(oa_c, ob_c, w_out, g_post[i, 1]))
    return xl.reshape(n_b, n_l, d)
```

```python
import functools
import math

import numpy as np
import jax
import jax.numpy as jnp
from jax import lax
from jax.experimental import pallas as pl
from jax.experimental.pallas import tpu as pltpu

F32 = jnp.float32
BF16 = jnp.bfloat16

N_MOD = 9
FFN_RES = 0.5
EPS = 1e-6
ROPE_THETA = 10000.0
GRID_W = 64
CONV_CH = 512
CONV_K = 31
MLA_HEADS = 8
MLA_NOPE = 64
MLA_ROPE = 32
MLA_V = 64
Q_LORA = 384
KV_LORA = 256
LOG2E = math.log2(math.e)
MLA_SCALE = (MLA_NOPE + MLA_ROPE) ** -0.5 * LOG2E
GQA_HEADS = 8
GQA_KV_HEADS = 2
GQA_GROUP = GQA_HEADS // GQA_KV_HEADS
GQA_HD = 64
GQA_SCALE = GQA_HD ** -0.5 * LOG2E
FNET_GROUPS = 4
FNET_CH = 128
HEADS = 8
HEAD_V = 64

LANE = 128
SUBLANE = 8
MXU_DIM = 256
VMEM_LIMIT = 56 * 1024 * 1024
FF_CHUNK = 256
FFN_ROWS = 1024
FFN_SUB = 512
CONV_HALO = 16
CONV_ROWS = 64
ATTN_KEYS = 256


def _cparams(n_axes):
    return pltpu.CompilerParams(dimension_semantics=("arbitrary",) * n_axes,
                                vmem_limit_bytes=VMEM_LIMIT)


def _resident(shape):
    zeros = (0,) * len(shape)
    return pl.BlockSpec(shape, lambda *_: zeros, pipeline_mode=pl.Buffered(1))


def _rms(x):
    return x * lax.rsqrt(jnp.mean(x * x, axis=-1, keepdims=True) + EPS)


def _sigmoid(x):
    return 1.0 / (1.0 + jnp.exp(-x))


def _dot(a, b):
    return jnp.dot(a, b, preferred_element_type=F32)


def _dot_t(a, b):
    return lax.dot_general(a, b, (((1,), (1,)), ((), ())), preferred_element_type=F32)


def _mod_kernel(c_ref, w_ref, b_ref, o_ref):
    c = c_ref[...]
    sc = (c * _sigmoid(c)).astype(BF16)
    o_ref[...] = _dot(sc, w_ref[...].astype(BF16)) + b_ref[...]


def _modulation(c_all, w_mod, b_mod):
    depth, d, nd = w_mod.shape
    bp = c_all.shape[0]
    tn = 1024
    return pl.pallas_call(
        _mod_kernel,
        grid=(depth, nd // tn),
        in_specs=[pl.BlockSpec((bp, d), lambda i, j: (0, 0)),
                  pl.BlockSpec((None, d, tn), lambda i, j: (i, 0, j)),
                  pl.BlockSpec((None, 1, tn), lambda i, j: (i, 0, j))],
        out_specs=pl.BlockSpec((None, bp, tn), lambda i, j: (i, 0, j)),
        out_shape=jax.ShapeDtypeStruct((depth, bp, nd), F32),
        compiler_params=_cparams(2),
        name="modulation",
    )(c_all, w_mod, b_mod.reshape(depth, 1, nd))


def _ffn_body(x, mod_ref, k, gpre_ref, gpost_ref, wg_ref, wu_ref, wd_ref, h_ref, u_ref, rows):
    d = x.shape[-1]
    shift = mod_ref[:, (3 * k) * d:(3 * k + 1) * d]
    scale = mod_ref[:, (3 * k + 1) * d:(3 * k + 2) * d]
    gate = mod_ref[:, (3 * k + 2) * d:(3 * k + 3) * d]
    h = _rms(x) * gpre_ref[...] * (1.0 + scale) + shift
    h_ref[rows, :] = h.astype(BF16)
    ch = FF_CHUNK
    for c in range(wg_ref.shape[1] // ch):
        a = _dot(h_ref[rows, :], wg_ref[:, c * ch:(c + 1) * ch])
        b = _dot(h_ref[rows, :], wu_ref[:, c * ch:(c + 1) * ch])
        u_ref[rows, c * ch:(c + 1) * ch] = (a * _sigmoid(a) * b).astype(BF16)
    y = _dot(u_ref[rows, :], wd_ref[...])
    return x + (FFN_RES * gate) * (_rms(y) * gpost_ref[...])


def _sub_tiles(n):
    sub = min(n, FFN_SUB)
    return [slice(r, r + sub) for r in range(0, n, sub)]


def _ffn_kernel(x_ref, mod_ref, gpre_ref, gpost_ref, wg_ref, wu_ref, wd_ref, o_ref, h_ref, u_ref,
                *, k):
    for rows in _sub_tiles(x_ref.shape[0]):
        o_ref[rows, :] = _ffn_body(x_ref[rows, :], mod_ref, k, gpre_ref, gpost_ref, wg_ref, wu_ref,
                                   wd_ref, h_ref, u_ref, rows)


def _dot_lhs_t(a_t, b):
    return lax.dot_general(a_t, b, (((0,), (0,)), ((), ())), preferred_element_type=F32)


def _mix_ffn_kernel(x_ref, oa_ref, ob_ref, wo_ref, gmix_ref, mod_ref, gpre_ref, gpost_ref,
                    wg_ref, wu_ref, wd_ref, o_ref, h_ref, u_ref, *, attn_first):
    d = x_ref.shape[-1]
    half = wo_ref.shape[0] // 2
    gate = mod_ref[:, 5 * d:6 * d]
    for rows in _sub_tiles(x_ref.shape[0]):
        if attn_first:
            y = _dot_lhs_t(oa_ref[:, rows], wo_ref[:half, :]) + _dot(ob_ref[rows, :], wo_ref[half:, :])
        else:
            y = _dot(oa_ref[rows, :], wo_ref[:half, :]) + _dot_lhs_t(ob_ref[:, rows], wo_ref[half:, :])
        x1 = x_ref[rows, :] + gate * (_rms(y) * gmix_ref[...])
        o_ref[rows, :] = _ffn_body(x1, mod_ref, 2, gpre_ref, gpost_ref, wg_ref, wu_ref, wd_ref,
                                   h_ref, u_ref, rows)


def _row_tile(rows_per_batch, total_rows, want=512):
    tm = min(want, rows_per_batch)
    assert rows_per_batch % tm == 0 and total_rows % tm == 0
    return tm


def _mod_spec(nd, rows_per_batch, tm, fixed_row):
    if fixed_row is not None:
        return pl.BlockSpec((None, 1, nd), lambda t: (fixed_row, 0, 0))
    per = rows_per_batch // tm
    return pl.BlockSpec((None, 1, nd), lambda t: (t // per, 0, 0))


def _layer_weight(w, i, j):
    return pl.BlockSpec((None, None) + w.shape[2:], lambda *_: (i, j, 0, 0),
                        pipeline_mode=pl.Buffered(1))


def _ffn(x, mod, k, g_pre, g_post, ffn_w, ij, rows_per_batch, fixed_row=None, mix=None):
    r, d = x.shape
    nd = mod.shape[-1]
    wg, wu, wd = ffn_w
    dff = wd.shape[2]
    tm = _row_tile(rows_per_batch, r, want=FFN_ROWS)
    row = lambda t: (t, 0)
    vec = pl.BlockSpec((1, d), lambda t: (0, 0))
    tail_specs = [_mod_spec(nd, rows_per_batch, tm, fixed_row), vec, vec,
                  _layer_weight(wg, *ij), _layer_weight(wu, *ij), _layer_weight(wd, *ij)]
    tail_args = [mod, g_pre.reshape(1, d), g_post.reshape(1, d), wg, wu, wd]
    scratch = [pltpu.VMEM((tm, d), BF16), pltpu.VMEM((tm, dff), BF16)]
    if mix is None:
        kern = functools.partial(_ffn_kernel, k=k)
        in_specs = [pl.BlockSpec((tm, d), row)] + tail_specs
        args = [x] + tail_args
        name = "ffn"
    else:
        oa, ob, wo, g_mix, attn_first = mix
        half = wo.shape[0] // 2
        kern = functools.partial(_mix_ffn_kernel, attn_first=attn_first)
        plain = pl.BlockSpec((tm, half), row)
        transposed = pl.BlockSpec((half, tm), lambda t: (0, t))
        in_specs = [pl.BlockSpec((tm, d), row), transposed if attn_first else plain,
                    plain if attn_first else transposed, _resident(wo.shape), vec] + tail_specs
        args = [x, oa, ob, wo, g_mix.reshape(1, d)] + tail_args
        name = "mix_ffn"
    return pl.pallas_call(
        kern,
        grid=(r // tm,),
        in_specs=in_specs,
        out_specs=pl.BlockSpec((tm, d), row),
        out_shape=jax.ShapeDtypeStruct((r, d), F32),
        scratch_shapes=scratch,
        compiler_params=_cparams(1),
        name=name,
    )(*args)


def _even_in_kernel(x_ref, mod_ref, gpre_ref, win_ref, gcq_ref, wq_ref, gckv_ref, wk_ref, wvt_ref,
                    tab_ref, u_ref, q_ref, k_ref, vt_ref):
    x = x_ref[...]
    d = x.shape[-1]
    shift = mod_ref[:, 3 * d:4 * d]
    scale = mod_ref[:, 4 * d:5 * d]
    h = (_rms(x) * gpre_ref[...] * (1.0 + scale) + shift).astype(BF16)
    p = _dot(h, win_ref[...])
    c0 = CONV_CH
    c1 = 2 * CONV_CH
    c2 = c1 + Q_LORA
    c3 = c2 + KV_LORA
    val = p[:, :c0]
    gat = p[:, c0:c1]
    u_ref[...] = val * _sigmoid(gat)
    cqn = (_rms(p[:, c1:c2]) * gcq_ref[...]).astype(BF16)
    ckvn = (_rms(p[:, c2:c3]) * gckv_ref[...]).astype(BF16)
    kpe = p[:, c3:c3 + LANE]
    kpe_sw = p[:, c3 + LANE:c3 + 2 * LANE]
    qa, qb, ka, kb = tab_ref[0], tab_ref[1], tab_ref[2], tab_ref[3]
    kpe_rot = kpe * ka + kpe_sw * kb
    qq = _dot(cqn, wq_ref[...])
    kn = _dot(ckvn, wk_ref[...])
    hw = MLA_HEADS * LANE
    for hd in range(MLA_HEADS):
        lo, hi = hd * LANE, (hd + 1) * LANE
        q_ref[:, lo:hi] = (qq[:, lo:hi] * qa + qq[:, hw + lo:hw + hi] * qb).astype(BF16)
        k_ref[:, lo:hi] = (kn[:, lo:hi] + kpe_rot).astype(BF16)
    vt_ref[...] = _dot_t(wvt_ref[...], ckvn).astype(BF16)


def _even_in(x, mod, g_pre, win, gcq, wq, gckv, wk, wvt, tabs, rows_per_batch, fixed_row=None):
    r, d = x.shape
    nd = mod.shape[-1]
    tm = _row_tile(rows_per_batch, r)
    per = rows_per_batch // tm
    row = lambda t: (t, 0)
    hw = MLA_HEADS * LANE
    if fixed_row is None:
        tab_spec = pl.BlockSpec((4, tm, LANE), lambda t: (0, t % per, 0))
    else:
        tab_spec = pl.BlockSpec((4, tm, LANE), lambda t: (0, 0, 0))
    return pl.pallas_call(
        _even_in_kernel,
        grid=(r // tm,),
        in_specs=[pl.BlockSpec((tm, d), row),
                  _mod_spec(nd, rows_per_batch, tm, fixed_row),
                  pl.BlockSpec((1, d), lambda t: (0, 0)),
                  _resident(win.shape),
                  pl.BlockSpec((1, Q_LORA), lambda t: (0, 0)),
                  _resident(wq.shape),
                  pl.BlockSpec((1, KV_LORA), lambda t: (0, 0)),
                  _resident(wk.shape),
                  _resident(wvt.shape),
                  tab_spec],
        out_specs=[pl.BlockSpec((tm, CONV_CH), row), pl.BlockSpec((tm, hw), row),
                   pl.BlockSpec((tm, hw), row), pl.BlockSpec((wvt.shape[0], tm), lambda t: (0, t))],
        out_shape=[jax.ShapeDtypeStruct((r, CONV_CH), F32), jax.ShapeDtypeStruct((r, hw), BF16),
                   jax.ShapeDtypeStruct((r, hw), BF16),
                   jax.ShapeDtypeStruct((wvt.shape[0], r), BF16)],
        compiler_params=_cparams(1),
        name="even_in",
    )(x, mod, g_pre.reshape(1, d), win, gcq.reshape(1, -1), wq, gckv.reshape(1, -1), wk, wvt, tabs)


def _head_mean_sq(x, g_ref):
    sq = x * x
    hi = sq.astype(BF16)
    lo = (sq - hi.astype(F32)).astype(BF16)
    return _dot(hi, g_ref[...]) + _dot(lo, g_ref[...])


def _odd_in_kernel(x_ref, mod_ref, gpre_ref, win_ref, wvt_ref, gmat_ref, dft_ref, tab_ref,
                   q_ref, k_ref, vt_ref, f_ref):
    x = x_ref[...]
    d = x.shape[-1]
    shift = mod_ref[:, 3 * d:4 * d]
    scale = mod_ref[:, 4 * d:5 * d]
    h = (_rms(x) * gpre_ref[...] * (1.0 + scale) + shift).astype(BF16)
    p = _dot(h, win_ref[...])
    nq = GQA_HEADS * GQA_HD
    nk = GQA_KV_HEADS * LANE
    o_qs = nq
    o_k = 2 * nq
    o_ks = o_k + nk
    o_f = o_ks + nk
    qa, qb, ka, kb = tab_ref[0], tab_ref[1], tab_ref[2], tab_ref[3]
    for c in range(nq // MXU_DIM):
        lo = c * MXU_DIM
        qc = p[:, lo:lo + MXU_DIM]
        qs = p[:, o_qs + lo:o_qs + lo + MXU_DIM]
        r = lax.rsqrt(_head_mean_sq(qc, gmat_ref) + EPS)
        for s in range(MXU_DIM // LANE):
            a, b = s * LANE, (s + 1) * LANE
            q_ref[:, lo + a:lo + b] = (r[:, a:b] * (qc[:, a:b] * qa + qs[:, a:b] * qb)).astype(BF16)
    kc = p[:, o_k:o_k + nk]
    ks = p[:, o_ks:o_ks + nk]
    r = lax.rsqrt(_head_mean_sq(kc, gmat_ref) + EPS)
    for s in range(nk // LANE):
        a, b = s * LANE, (s + 1) * LANE
        k_ref[:, a:b] = (r[:, a:b] * (kc[:, a:b] * ka + ks[:, a:b] * kb)).astype(BF16)
    vt_ref[...] = _dot_t(wvt_ref[...], h).astype(BF16)
    nf = FNET_GROUPS * FNET_CH
    for c in range(nf // MXU_DIM):
        lo = c * MXU_DIM
        fx = p[:, o_f + lo:o_f + lo + MXU_DIM].astype(BF16)
        cs = _dot(fx, dft_ref[...])
        f_ref[:, lo:lo + MXU_DIM] = cs[:, :MXU_DIM].astype(BF16)
        f_ref[:, nf + lo:nf + lo + MXU_DIM] = cs[:, MXU_DIM:].astype(BF16)


def _odd_in(x, mod, g_pre, win, wvt, gmat, dft, tabs, rows_per_batch, fixed_row=None):
    r, d = x.shape
    nd = mod.shape[-1]
    tm = _row_tile(rows_per_batch, r)
    per = rows_per_batch // tm
    row = lambda t: (t, 0)
    nq = GQA_HEADS * GQA_HD
    nk = GQA_KV_HEADS * LANE
    nf = FNET_GROUPS * FNET_CH
    if fixed_row is None:
        tab_spec = pl.BlockSpec((4, tm, LANE), lambda t: (0, t % per, 0))
    else:
        tab_spec = pl.BlockSpec((4, tm, LANE), lambda t: (0, 0, 0))
    return pl.pallas_call(
        _odd_in_kernel,
        grid=(r // tm,),
        in_specs=[pl.BlockSpec((tm, d), row),
                  _mod_spec(nd, rows_per_batch, tm, fixed_row),
                  pl.BlockSpec((1, d), lambda t: (0, 0)),
                  _resident(win.shape),
                  _resident(wvt.shape),
                  _resident(gmat.shape),
                  _resident(dft.shape),
                  tab_spec],
        out_specs=[pl.BlockSpec((tm, nq), row), pl.BlockSpec((tm, nk), row),
                   pl.BlockSpec((wvt.shape[0], tm), lambda t: (0, t)),
                   pl.BlockSpec((tm, 2 * nf), row)],
        out_shape=[jax.ShapeDtypeStruct((r, nq), BF16), jax.ShapeDtypeStruct((r, nk), BF16),
                   jax.ShapeDtypeStruct((wvt.shape[0], r), BF16),
                   jax.ShapeDtypeStruct((r, 2 * nf), BF16)],
        compiler_params=_cparams(1),
        name="odd_in",
    )(x, mod, g_pre.reshape(1, d), win, wvt, gmat, dft, tabs)


def _attn_kernel(*refs, n_src, q_slab, q_half, k_slab, v_row):
    q_ref = refs[0]
    k_refs = refs[1:1 + n_src]
    vt_refs = refs[1 + n_src:1 + 2 * n_src]
    o_ref = refs[1 + 2 * n_src]
    s_ref = refs[2 + 2 * n_src]
    tq = q_ref.shape[0]
    lane = lax.broadcasted_iota(jnp.int32, (tq, LANE), 1)
    starts = [0]
    for kr in k_refs:
        starts.append(starts[-1] + kr.shape[0])
    slot = lambda hd: hd % 2

    def scores(hd):
        q = q_ref[:, q_slab[hd] * LANE:(q_slab[hd] + 1) * LANE]
        if q_half[hd] is not None:
            keep = (lane < HEAD_V) if q_half[hd] == 0 else (lane >= HEAD_V)
            q = jnp.where(keep, q, jnp.zeros_like(q))
        ks = k_slab[hd]
        m = None
        for kr, r0 in zip(k_refs, starts):
            si = _dot_t(kr[:, ks * LANE:(ks + 1) * LANE], q)
            s_ref[slot(hd), r0:r0 + kr.shape[0], :] = si
            mi = si.max(axis=0, keepdims=True)
            m = mi if m is None else jnp.maximum(m, mi)
        return m

    m_next = scores(0)
    for hd in range(HEADS):
        m = m_next
        if hd + 1 < HEADS:
            m_next = scores(hd + 1)
        v0 = v_row[hd] * HEAD_V
        den = None
        o = None
        for vr, r0 in zip(vt_refs, starts):
            for c0 in range(0, vr.shape[1], ATTN_KEYS):
                c1 = min(c0 + ATTN_KEYS, vr.shape[1])
                pc = jnp.exp2(s_ref[slot(hd), r0 + c0:r0 + c1, :] - m)
                dc = pc.sum(axis=0, keepdims=True)
                oc = _dot(vr[v0:v0 + HEAD_V, c0:c1], pc.astype(BF16))
                den = dc if den is None else den + dc
                o = oc if o is None else o + oc
        o_ref[hd * HEAD_V:(hd + 1) * HEAD_V, :] = (o / den).astype(BF16)


def _attention(q, kvs, n_batch, lq, head_maps, tq=512):
    tq = min(tq, lq)
    per = lq // tq
    n_src = len(kvs)
    wq = q.shape[1]
    in_specs = [pl.BlockSpec((tq, wq), lambda b, i: (b * per + i, 0))]
    for k, _, lk in kvs:
        in_specs.append(pl.BlockSpec((lk, k.shape[1]), lambda b, i: (b, 0)))
    for _, vt, lk in kvs:
        in_specs.append(pl.BlockSpec((vt.shape[0], lk), lambda b, i: (0, b)))
    kern = functools.partial(_attn_kernel, n_src=n_src, **head_maps)
    return pl.pallas_call(
        kern,
        grid=(n_batch, per),
        in_specs=in_specs,
        out_specs=pl.BlockSpec((HEADS * HEAD_V, tq), lambda b, i: (0, b * per + i)),
        out_shape=jax.ShapeDtypeStruct((HEADS * HEAD_V, n_batch * lq), BF16),
        scratch_shapes=[pltpu.VMEM((2, sum(lk for _, _, lk in kvs), tq), F32)],
        compiler_params=_cparams(2),
        name="attention",
    )(q, *[k for k, _, _ in kvs], *[vt for _, vt, _ in kvs])


_MLA_MAPS = dict(q_slab=tuple(range(8)), q_half=(None,) * 8,
                 k_slab=tuple(range(8)), v_row=tuple(range(8)))
_GQA_MAPS = dict(q_slab=tuple(h // 2 for h in range(8)), q_half=tuple(h % 2 for h in range(8)),
                 k_slab=tuple(h // GQA_GROUP for h in range(8)),
                 v_row=tuple(h // GQA_GROUP for h in range(8)))


def _conv_kernel(u_ref, w_ref, b_ref, g_ref, beta_ref, o_ref, pad_ref):
    n, ch = u_ref.shape
    zeros = jnp.zeros((CONV_HALO, ch), F32)
    pad_ref[0:CONV_HALO, :] = zeros
    pad_ref[CONV_HALO + n:2 * CONV_HALO + n, :] = zeros
    pad_ref[CONV_HALO:CONV_HALO + n, :] = u_ref[...]
    first = CONV_HALO - CONV_K // 2
    span = CONV_ROWS + 2 * CONV_HALO

    def step(i, carry):
        r0 = pl.multiple_of(i * CONV_ROWS, CONV_ROWS)
        cols = []
        for c0 in range(0, ch, LANE):
            win = pad_ref[pl.ds(r0, span), c0:c0 + LANE]
            acc = None
            for s in range(SUBLANE):
                sh = win if s == 0 else pltpu.roll(win, span - s, axis=0)
                sh = sh.reshape(span // SUBLANE, SUBLANE, LANE)
                for k in range(CONV_K):
                    a, sk = divmod(first + k, SUBLANE)
                    if sk == s:
                        term = sh[a:a + CONV_ROWS // SUBLANE] * w_ref[k, :, c0:c0 + LANE][None]
                        acc = term if acc is None else acc + term
            cols.append(acc.reshape(CONV_ROWS, LANE) + b_ref[:, c0:c0 + LANE])
        v = jnp.concatenate(cols, axis=1)
        mu = jnp.mean(v, axis=-1, keepdims=True)
        cen = v - mu
        var = jnp.mean(cen * cen, axis=-1, keepdims=True)
        y = cen * lax.rsqrt(var + EPS) * g_ref[...] + beta_ref[...]
        o_ref[pl.ds(r0, CONV_ROWS), :] = (y * _sigmoid(y)).astype(BF16)
        return carry

    lax.fori_loop(0, n // CONV_ROWS, step, 0)


def _conformer_conv(u, n_batch, n, conv_w, conv_b, ln_g, ln_b):
    ch = u.shape[-1]
    vec = pl.BlockSpec((1, ch), lambda b: (0, 0))
    return pl.pallas_call(
        _conv_kernel,
        grid=(n_batch,),
        in_specs=[pl.BlockSpec((n, ch), lambda b: (b, 0)),
                  pl.BlockSpec((CONV_K, SUBLANE, ch), lambda b: (0, 0, 0)), vec, vec, vec],
        out_specs=pl.BlockSpec((n, ch), lambda b: (b, 0)),
        out_shape=jax.ShapeDtypeStruct((n_batch * n, ch), BF16),
        scratch_shapes=[pltpu.VMEM((n + 2 * CONV_HALO, ch), F32)],
        compiler_params=_cparams(1),
        name="conformer_conv",
    )(u, jnp.broadcast_to(conv_w[:, None, :], (CONV_K, SUBLANE, ch)),
      conv_b.reshape(1, ch), ln_g.reshape(1, ch), ln_b.reshape(1, ch))


def _seq_dft_kernel(x_ref, c_ref, s_ref, o_ref, *, norm):
    tr = o_ref.shape[0]
    nf = o_ref.shape[1]
    r0 = pl.multiple_of(pl.program_id(1) * tr, tr)
    acc = _dot(c_ref[pl.ds(r0, tr), :], x_ref[:, :nf]) - _dot(s_ref[pl.ds(r0, tr), :], x_ref[:, nf:])
    o_ref[...] = (acc * norm).astype(BF16)


def _seq_dft(xf, n_batch, n, cmat, smat):
    nf = xf.shape[1] // 2
    tr = min(512, n)
    per = n // tr
    norm = 1.0 / math.sqrt(n * FNET_CH)
    return pl.pallas_call(
        functools.partial(_seq_dft_kernel, norm=norm),
        grid=(n_batch, per),
        in_specs=[pl.BlockSpec((n, 2 * nf), lambda b, j: (b, 0)),
                  _resident(cmat.shape), _resident(smat.shape)],
        out_specs=pl.BlockSpec((tr, nf), lambda b, j: (b * per + j, 0)),
        out_shape=jax.ShapeDtypeStruct((n_batch * n, nf), BF16),
        compiler_params=_cparams(2),
        name="seq_dft",
    )(xf, cmat, smat)


def _rope_cos_sin(n, rot_dim):
    nf = rot_dim // 4
    t = jnp.arange(n)
    row = (t // GRID_W).astype(F32)
    col = (t % GRID_W).astype(F32)
    inv = ROPE_THETA ** (-jnp.arange(nf, dtype=F32) / nf)
    ang = jnp.concatenate([row[:, None] * inv, col[:, None] * inv], axis=-1)
    cos, sin = jnp.cos(ang), jnp.sin(ang)
    lanes = np.arange(rot_dim)
    half = (lanes % (2 * nf)) // nf
    src = (lanes // (2 * nf)) * nf + lanes % nf
    sign = np.where(half == 0, -1.0, 1.0).astype(np.float32)
    partner = np.where(half == 0, lanes + nf, lanes - nf)
    return cos[:, src], sin[:, src] * sign, partner


def _mla_tables(n, tm_ctx):
    cos, sin, partner = _rope_cos_sin(n, MLA_ROPE)
    ones = jnp.ones((n, MLA_NOPE), F32)
    zer = jnp.zeros((n, MLA_NOPE), F32)
    pad = jnp.zeros((n, LANE - MLA_NOPE - MLA_ROPE), F32)
    qa = jnp.concatenate([ones, cos, pad], axis=1) * MLA_SCALE
    qb = jnp.concatenate([zer, sin, pad], axis=1) * MLA_SCALE
    ka = jnp.concatenate([zer, cos, pad], axis=1)
    kb = jnp.concatenate([zer, sin, pad], axis=1)
    lane = np.arange(LANE)
    c_qa = np.where(lane < MLA_NOPE + MLA_ROPE, MLA_SCALE, 0.0).astype(np.float32)
    c_ka = np.where((lane >= MLA_NOPE) & (lane < MLA_NOPE + MLA_ROPE), 1.0, 0.0).astype(np.float32)
    ctx = [jnp.broadcast_to(jnp.asarray(c), (tm_ctx, LANE))
           for c in (c_qa, np.zeros(LANE, np.float32), c_ka, np.zeros(LANE, np.float32))]
    lat = jnp.stack([qa, qb, ka, kb])
    return lat, jnp.stack(ctx), partner


def _gqa_tables(n, tm_ctx, g_qn, g_kn):
    cos, sin, partner = _rope_cos_sin(n, GQA_HD)
    two = lambda a: jnp.concatenate([a, a], axis=-1)
    qa = two(cos * g_qn) * GQA_SCALE
    qb = two(sin * g_qn[partner]) * GQA_SCALE
    ka = two(cos * g_kn)
    kb = two(sin * g_kn[partner])
    lat = jnp.stack([qa, qb, ka, kb])
    zero = jnp.zeros((tm_ctx, LANE), F32)
    ctx = jnp.stack([jnp.broadcast_to(two(g_qn) * GQA_SCALE, (tm_ctx, LANE)), zero,
                     jnp.broadcast_to(two(g_kn), (tm_ctx, LANE)), zero])
    return lat, ctx, partner


def _even_weights(w_in, w_uq, w_ukv, partner):
    d = w_in.shape[0]
    a_cols = 2 * CONV_CH
    kpe0 = a_cols + Q_LORA + KV_LORA
    rope_lo = MLA_NOPE
    zpad = lambda n: jnp.zeros((d, n), F32)
    kpe = w_in[:, kpe0:kpe0 + MLA_ROPE]
    kpe_slab = jnp.concatenate([zpad(rope_lo), kpe, zpad(LANE - rope_lo - MLA_ROPE)], axis=1)
    kpe_sw_slab = jnp.concatenate([zpad(rope_lo), kpe[:, partner],
                                   zpad(LANE - rope_lo - MLA_ROPE)], axis=1)
    win = jnp.concatenate([w_in[:, :kpe0], kpe_slab, kpe_sw_slab], axis=1).astype(BF16)

    dk = MLA_NOPE + MLA_ROPE
    wq3 = w_uq.reshape(Q_LORA, MLA_HEADS, dk)
    qz = jnp.zeros((Q_LORA, MLA_HEADS, LANE - dk), F32)
    wq_pad = jnp.concatenate([wq3, qz], axis=2)
    wq_sw = jnp.concatenate([jnp.zeros((Q_LORA, MLA_HEADS, MLA_NOPE), F32),
                             wq3[:, :, MLA_NOPE:][:, :, partner], qz], axis=2)
    wq = jnp.concatenate([wq_pad.reshape(Q_LORA, -1), wq_sw.reshape(Q_LORA, -1)], axis=1).astype(BF16)

    wkv3 = w_ukv.reshape(KV_LORA, MLA_HEADS, MLA_NOPE + MLA_V)
    kz = jnp.zeros((KV_LORA, MLA_HEADS, LANE - MLA_NOPE), F32)
    wk = jnp.concatenate([wkv3[:, :, :MLA_NOPE], kz], axis=2).reshape(KV_LORA, -1).astype(BF16)
    wvt = wkv3[:, :, MLA_NOPE:].reshape(KV_LORA, -1).T.astype(BF16)
    return win, wq, wk, wvt


def _odd_weights(w_in, partner):
    d = w_in.shape[0]
    nq = GQA_HEADS * GQA_HD
    nkv = GQA_KV_HEADS * GQA_HD
    wq = w_in[:, :nq].reshape(d, GQA_HEADS, GQA_HD)
    wk = w_in[:, nq:nq + nkv].reshape(d, GQA_KV_HEADS, GQA_HD)
    wv = w_in[:, nq + nkv:nq + 2 * nkv].reshape(d, GQA_KV_HEADS, GQA_HD)
    wf = w_in[:, nq + 2 * nkv:]
    dup = lambda a: jnp.concatenate([a, a], axis=2).reshape(d, -1)
    win = jnp.concatenate([wq.reshape(d, -1), wq[:, :, partner].reshape(d, -1),
                           dup(wk), dup(wk[:, :, partner]), wf], axis=1).astype(BF16)
    return win, wv.reshape(d, -1).T.astype(BF16)


def _head_mean_matrix():
    i = np.arange(MXU_DIM)
    same = (i[:, None] // GQA_HD) == (i[None, :] // GQA_HD)
    return jnp.asarray(np.where(same, 1.0 / GQA_HD, 0.0), BF16)


def _channel_dft_matrix():
    i = np.arange(MXU_DIM)
    ang = 2.0 * np.pi * ((i[:, None] % FNET_CH) * (i[None, :] % FNET_CH) % FNET_CH) / FNET_CH
    same = (i[:, None] // FNET_CH) == (i[None, :] // FNET_CH)
    c = np.where(same, np.cos(ang), 0.0)
    s = np.where(same, np.sin(ang), 0.0)
    return jnp.asarray(np.concatenate([c, s], axis=1), BF16)


def _seq_dft_matrices(n):
    i = jnp.arange(n, dtype=jnp.int32)
    ang = ((i[:, None] * i[None, :]) % n).astype(F32) * (2.0 * math.pi / n)
    return jnp.cos(ang).astype(BF16), jnp.sin(ang).astype(BF16)


def kernel(x, c, ctx, c_ctx, w_mod, b_mod, g_pre, g_post, ffn_w_gate, ffn_w_up, ffn_w_down,
           ev_w_in, ev_conv_w, ev_conv_b, ev_ln_g, ev_ln_b, ev_g_cq, ev_w_uq, ev_g_ckv,
           ev_w_ukv, ev_w_out, od_w_in, od_g_qn, od_g_kn, od_w_out):
    n_b, n_l, d = x.shape
    n_c = ctx.shape[1]
    depth = w_mod.shape[0]
    nd = w_mod.shape[2]

    bp = -(-(n_b + 1) // SUBLANE) * SUBLANE
    c_all = jnp.concatenate([c, c_ctx[None, :], jnp.zeros((bp - n_b - 1, d), F32)], axis=0)
    mod_all = _modulation(c_all, w_mod, b_mod).reshape(depth, bp, 1, nd)

    xl = x.reshape(n_b * n_l, d)
    xc = ctx.reshape(n_b * n_c, d)
    ctx_rows = n_b * n_c
    tm_c = _row_tile(ctx_rows, ctx_rows)

    mla_lat, mla_ctx, mla_partner = _mla_tables(n_l, tm_c)
    gmat = _head_mean_matrix()
    dft_ch = _channel_dft_matrix()
    cmat_l, smat_l = _seq_dft_matrices(n_l)
    cmat_c, smat_c = _seq_dft_matrices(n_c)

    ffn_w = (ffn_w_gate.astype(BF16), ffn_w_up.astype(BF16), ffn_w_down.astype(BF16))

    for i in range(depth):
        need_ctx = i < depth - 1
        mod = mod_all[i]
        j = i // 2
        xl = _ffn(xl, mod, 0, g_pre[i, 0], g_post[i, 0], ffn_w, (i, 0), rows_per_batch=n_l)
        xc = _ffn(xc, mod, 0, g_pre[i, 0], g_post[i, 0], ffn_w, (i, 0), rows_per_batch=ctx_rows,
                  fixed_row=n_b)
        if i % 2 == 0:
            win, wq, wk, wvt = _even_weights(ev_w_in[j], ev_w_uq[j], ev_w_ukv[j], mla_partner)
            proj = (g_pre[i, 1], win, ev_g_cq[j], wq, ev_g_ckv[j], wk, wvt)
            u_l, q_l, k_l, v_l = _even_in(xl, mod, *proj, mla_lat, rows_per_batch=n_l)
            u_c, q_c, k_c, v_c = _even_in(xc, mod, *proj, mla_ctx, rows_per_batch=ctx_rows,
                                          fixed_row=n_b)
            conv = (ev_conv_w[j], ev_conv_b[j], ev_ln_g[j], ev_ln_b[j])
            ob_l = _attention(q_l, [(k_c, v_c, n_c), (k_l, v_l, n_l)], n_b, n_l, _MLA_MAPS)
            oa_l = _conformer_conv(u_l, n_b, n_l, *conv)
            if need_ctx:
                ob_c = _attention(q_c, [(k_c, v_c, n_c)], n_b, n_c, _MLA_MAPS)
                oa_c = _conformer_conv(u_c, n_b, n_c, *conv)
            w_out = ev_w_out[j].astype(BF16)
        else:
            gqa_lat, gqa_ctx, gqa_partner = _gqa_tables(n_l, tm_c, od_g_qn[j], od_g_kn[j])
            win, wvt = _odd_weights(od_w_in[j], gqa_partner)
            proj = (g_pre[i, 1], win, wvt, gmat, dft_ch)
            q_l, k_l, v_l, f_l = _odd_in(xl, mod, *proj, gqa_lat, rows_per_batch=n_l)
            q_c, k_c, v_c, f_c = _odd_in(xc, mod, *proj, gqa_ctx, rows_per_batch=ctx_rows,
                                         fixed_row=n_b)
            oa_l = _attention(q_l, [(k_c, v_c, n_c), (k_l, v_l, n_l)], n_b, n_l, _GQA_MAPS)
            ob_l = _seq_dft(f_l, n_b, n_l, cmat_l, smat_l)
            if need_ctx:
                oa_c = _attention(q_c, [(k_c, v_c, n_c)], n_b, n_c, _GQA_MAPS)
                ob_c = _seq_dft(f_c, n_b, n_c, cmat_c, smat_c)
            w_out = od_w_out[j].astype(BF16)
        xl = _ffn(xl, mod, 2, g_pre[i, 2], g_post[i, 2], ffn_w, (i, 1), rows_per_batch=n_l,
                  mix=(oa_l, ob_l, w_out, g_post[i, 1], i % 2 == 1))
        if need_ctx:
            xc = _ffn(xc, mod, 2, g_pre[i, 2], g_post[i, 2], ffn_w, (i, 1), rows_per_batch=ctx_rows,
                      fixed_row=n_b, mix=(oa_c, ob_c, w_out, g_post[i, 1], i % 2 == 1))
    return xl.reshape(n_b, n_l, d)
```

```python
import functools
import math

import numpy as np
import jax
import jax.numpy as jnp
from jax import lax
from jax.experimental import pallas as pl
from jax.experimental.pallas import tpu as pltpu

F32 = jnp.float32
BF16 = jnp.bfloat16

N_MOD = 9
FFN_RES = 0.5
EPS = 1e-6
ROPE_THETA = 10000.0
GRID_W = 64
CONV_CH = 512
CONV_K = 31
MLA_HEADS = 8
MLA_NOPE = 64
MLA_ROPE = 32
MLA_V = 64
Q_LORA = 384
KV_LORA = 256
LOG2E = math.log2(math.e)
MLA_SCALE = (MLA_NOPE + MLA_ROPE) ** -0.5 * LOG2E
GQA_HEADS = 8
GQA_KV_HEADS = 2
GQA_GROUP = GQA_HEADS // GQA_KV_HEADS
GQA_HD = 64
GQA_SCALE = GQA_HD ** -0.5 * LOG2E
FNET_GROUPS = 4
FNET_CH = 128
HEADS = 8
HEAD_V = 64

LANE = 128
SUBLANE = 8
MXU_DIM = 256
VMEM_LIMIT = 56 * 1024 * 1024
FF_CHUNK = 256
FFN_ROWS = 1024
FFN_SUB = 512
CONV_HALO = 16
CONV_ROWS = 128
ATTN_KEYS = 256


def _cparams(n_axes):
    return pltpu.CompilerParams(dimension_semantics=("arbitrary",) * n_axes,
                                vmem_limit_bytes=VMEM_LIMIT)


def _resident(shape):
    zeros = (0,) * len(shape)
    return pl.BlockSpec(shape, lambda *_: zeros, pipeline_mode=pl.Buffered(1))


def _rms(x):
    return x * lax.rsqrt(jnp.mean(x * x, axis=-1, keepdims=True) + EPS)


def _sigmoid(x):
    return 1.0 / (1.0 + jnp.exp(-x))


def _dot(a, b):
    return jnp.dot(a, b, preferred_element_type=F32)


def _dot_t(a, b):
    return lax.dot_general(a, b, (((1,), (1,)), ((), ())), preferred_element_type=F32)


def _mod_kernel(c_ref, w_ref, b_ref, o_ref):
    c = c_ref[...]
    sc = (c * _sigmoid(c)).astype(BF16)
    o_ref[...] = _dot(sc, w_ref[...].astype(BF16)) + b_ref[...]


def _modulation(c_all, w_mod, b_mod):
    depth, d, nd = w_mod.shape
    bp = c_all.shape[0]
    tn = 1024
    return pl.pallas_call(
        _mod_kernel,
        grid=(depth, nd // tn),
        in_specs=[pl.BlockSpec((bp, d), lambda i, j: (0, 0)),
                  pl.BlockSpec((None, d, tn), lambda i, j: (i, 0, j)),
                  pl.BlockSpec((None, 1, tn), lambda i, j: (i, 0, j))],
        out_specs=pl.BlockSpec((None, bp, tn), lambda i, j: (i, 0, j)),
        out_shape=jax.ShapeDtypeStruct((depth, bp, nd), F32),
        compiler_params=_cparams(2),
        name="modulation",
    )(c_all, w_mod, b_mod.reshape(depth, 1, nd))


def _ffn_steps(x_of, mod_ref, k, gpre_ref, gpost_ref, wg_ref, wu_ref, wd_ref, o_ref, h_ref, u_ref):
    d = o_ref.shape[-1]
    shift = mod_ref[:, (3 * k) * d:(3 * k + 1) * d]
    scale = mod_ref[:, (3 * k + 1) * d:(3 * k + 2) * d]
    gate = mod_ref[:, (3 * k + 2) * d:(3 * k + 3) * d]
    ch = FF_CHUNK
    n_chunks = wg_ref.shape[1] // ch
    sub = min(o_ref.shape[0], FFN_SUB)
    tiles = [slice(r, r + sub) for r in range(0, o_ref.shape[0], sub)]

    def prologue(rows):
        x = x_of(rows)
        o_ref[rows, :] = x
        h_ref[rows, :] = (_rms(x) * gpre_ref[...] * (1.0 + scale) + shift).astype(BF16)

    def up(rows, c):
        a = _dot(h_ref[rows, :], wg_ref[:, c * ch:(c + 1) * ch])
        b = _dot(h_ref[rows, :], wu_ref[:, c * ch:(c + 1) * ch])
        u_ref[rows, c * ch:(c + 1) * ch] = (a * _sigmoid(a) * b).astype(BF16)

    def epilogue(rows, y):
        o_ref[rows, :] = o_ref[rows, :] + (FFN_RES * gate) * (_rms(y) * gpost_ref[...])

    prologue(tiles[0])
    up(tiles[0], 0)
    for t, rows in enumerate(tiles):
        nxt = tiles[t + 1] if t + 1 < len(tiles) else None
        if nxt is not None:
            prologue(nxt)
        for c in range(1, n_chunks):
            up(rows, c)
        y = _dot(u_ref[rows, :], wd_ref[...])
        if nxt is not None:
            up(nxt, 0)
        epilogue(rows, y)


def _ffn_kernel(x_ref, mod_ref, gpre_ref, gpost_ref, wg_ref, wu_ref, wd_ref, o_ref, h_ref, u_ref,
                *, k):
    _ffn_steps(lambda rows: x_ref[rows, :], mod_ref, k, gpre_ref, gpost_ref, wg_ref, wu_ref, wd_ref,
               o_ref, h_ref, u_ref)


def _dot_lhs_t(a_t, b):
    return lax.dot_general(a_t, b, (((0,), (0,)), ((), ())), preferred_element_type=F32)


def _mix_ffn_kernel(x_ref, oa_ref, ob_ref, wo_ref, gmix_ref, mod_ref, gpre_ref, gpost_ref,
                    wg_ref, wu_ref, wd_ref, o_ref, h_ref, u_ref, *, attn_first):
    d = x_ref.shape[-1]
    half = wo_ref.shape[0] // 2
    gate = mod_ref[:, 5 * d:6 * d]

    def mixed(rows):
        if attn_first:
            y = _dot_lhs_t(oa_ref[:, rows], wo_ref[:half, :]) + _dot(ob_ref[rows, :], wo_ref[half:, :])
        else:
            y = _dot(oa_ref[rows, :], wo_ref[:half, :]) + _dot_lhs_t(ob_ref[:, rows], wo_ref[half:, :])
        return x_ref[rows, :] + gate * (_rms(y) * gmix_ref[...])

    _ffn_steps(mixed, mod_ref, 2, gpre_ref, gpost_ref, wg_ref, wu_ref, wd_ref, o_ref, h_ref, u_ref)


def _row_tile(rows_per_batch, total_rows, want=512):
    tm = min(want, rows_per_batch)
    assert rows_per_batch % tm == 0 and total_rows % tm == 0
    return tm


def _mod_spec(nd, rows_per_batch, tm, fixed_row):
    if fixed_row is not None:
        return pl.BlockSpec((None, 1, nd), lambda t: (fixed_row, 0, 0))
    per = rows_per_batch // tm
    return pl.BlockSpec((None, 1, nd), lambda t: (t // per, 0, 0))


def _layer_weight(w, i, j):
    return pl.BlockSpec((None, None) + w.shape[2:], lambda *_: (i, j, 0, 0),
                        pipeline_mode=pl.Buffered(1))


def _ffn(x, mod, k, g_pre, g_post, ffn_w, ij, rows_per_batch, fixed_row=None, mix=None):
    r, d = x.shape
    nd = mod.shape[-1]
    wg, wu, wd = ffn_w
    dff = wd.shape[2]
    tm = _row_tile(rows_per_batch, r, want=FFN_ROWS)
    row = lambda t: (t, 0)
    vec = pl.BlockSpec((1, d), lambda t: (0, 0))
    tail_specs = [_mod_spec(nd, rows_per_batch, tm, fixed_row), vec, vec,
                  _layer_weight(wg, *ij), _layer_weight(wu, *ij), _layer_weight(wd, *ij)]
    tail_args = [mod, g_pre.reshape(1, d), g_post.reshape(1, d), wg, wu, wd]
    scratch = [pltpu.VMEM((tm, d), BF16), pltpu.VMEM((tm, dff), BF16)]
    if mix is None:
        kern = functools.partial(_ffn_kernel, k=k)
        in_specs = [pl.BlockSpec((tm, d), row)] + tail_specs
        args = [x] + tail_args
        name = "ffn"
    else:
        oa, ob, wo, g_mix, attn_first = mix
        half = wo.shape[0] // 2
        kern = functools.partial(_mix_ffn_kernel, attn_first=attn_first)
        plain = pl.BlockSpec((tm, half), row)
        transposed = pl.BlockSpec((half, tm), lambda t: (0, t))
        in_specs = [pl.BlockSpec((tm, d), row), transposed if attn_first else plain,
                    plain if attn_first else transposed, _resident(wo.shape), vec] + tail_specs
        args = [x, oa, ob, wo, g_mix.reshape(1, d)] + tail_args
        name = "mix_ffn"
    return pl.pallas_call(
        kern,
        grid=(r // tm,),
        in_specs=in_specs,
        out_specs=pl.BlockSpec((tm, d), row),
        out_shape=jax.ShapeDtypeStruct((r, d), F32),
        scratch_shapes=scratch,
        compiler_params=_cparams(1),
        name=name,
    )(*args)


def _even_in_kernel(x_ref, mod_ref, gpre_ref, win_ref, gcq_ref, wq_ref, gckv_ref, wk_ref, wvt_ref,
                    tab_ref, u_ref, q_ref, k_ref, vt_ref):
    x = x_ref[...]
    d = x.shape[-1]
    shift = mod_ref[:, 3 * d:4 * d]
    scale = mod_ref[:, 4 * d:5 * d]
    h = (_rms(x) * gpre_ref[...] * (1.0 + scale) + shift).astype(BF16)
    p = _dot(h, win_ref[...])
    c0 = CONV_CH
    c1 = 2 * CONV_CH
    c2 = c1 + Q_LORA
    c3 = c2 + KV_LORA
    val = p[:, :c0]
    gat = p[:, c0:c1]
    u_ref[...] = val * _sigmoid(gat)
    cqn = (_rms(p[:, c1:c2]) * gcq_ref[...]).astype(BF16)
    ckvn = (_rms(p[:, c2:c3]) * gckv_ref[...]).astype(BF16)
    kpe = p[:, c3:c3 + LANE]
    kpe_sw = p[:, c3 + LANE:c3 + 2 * LANE]
    qa, qb, ka, kb = tab_ref[0], tab_ref[1], tab_ref[2], tab_ref[3]
    kpe_rot = kpe * ka + kpe_sw * kb
    qq = _dot(cqn, wq_ref[...])
    kn = _dot(ckvn, wk_ref[...])
    hw = MLA_HEADS * LANE
    for hd in range(MLA_HEADS):
        lo, hi = hd * LANE, (hd + 1) * LANE
        q_ref[:, lo:hi] = (qq[:, lo:hi] * qa + qq[:, hw + lo:hw + hi] * qb).astype(BF16)
        k_ref[:, lo:hi] = (kn[:, lo:hi] + kpe_rot).astype(BF16)
    vt_ref[...] = _dot_t(wvt_ref[...], ckvn).astype(BF16)


def _even_in(x, mod, g_pre, win, gcq, wq, gckv, wk, wvt, tabs, rows_per_batch, fixed_row=None):
    r, d = x.shape
    nd = mod.shape[-1]
    tm = _row_tile(rows_per_batch, r)
    per = rows_per_batch // tm
    row = lambda t: (t, 0)
    hw = MLA_HEADS * LANE
    if fixed_row is None:
        tab_spec = pl.BlockSpec((4, tm, LANE), lambda t: (0, t % per, 0))
    else:
        tab_spec = pl.BlockSpec((4, tm, LANE), lambda t: (0, 0, 0))
    return pl.pallas_call(
        _even_in_kernel,
        grid=(r // tm,),
        in_specs=[pl.BlockSpec((tm, d), row),
                  _mod_spec(nd, rows_per_batch, tm, fixed_row),
                  pl.BlockSpec((1, d), lambda t: (0, 0)),
                  _resident(win.shape),
                  pl.BlockSpec((1, Q_LORA), lambda t: (0, 0)),
                  _resident(wq.shape),
                  pl.BlockSpec((1, KV_LORA), lambda t: (0, 0)),
                  _resident(wk.shape),
                  _resident(wvt.shape),
                  tab_spec],
        out_specs=[pl.BlockSpec((tm, CONV_CH), row), pl.BlockSpec((tm, hw), row),
                   pl.BlockSpec((tm, hw), row), pl.BlockSpec((wvt.shape[0], tm), lambda t: (0, t))],
        out_shape=[jax.ShapeDtypeStruct((r, CONV_CH), F32), jax.ShapeDtypeStruct((r, hw), BF16),
                   jax.ShapeDtypeStruct((r, hw), BF16),
                   jax.ShapeDtypeStruct((wvt.shape[0], r), BF16)],
        compiler_params=_cparams(1),
        name="even_in",
    )(x, mod, g_pre.reshape(1, d), win, gcq.reshape(1, -1), wq, gckv.reshape(1, -1), wk, wvt, tabs)


def _head_mean_sq(x, g_ref):
    sq = x * x
    hi = sq.astype(BF16)
    lo = (sq - hi.astype(F32)).astype(BF16)
    return _dot(hi, g_ref[...]) + _dot(lo, g_ref[...])


def _odd_in_kernel(x_ref, mod_ref, gpre_ref, win_ref, wvt_ref, gmat_ref, dft_ref, tab_ref,
                   q_ref, k_ref, vt_ref, f_ref):
    x = x_ref[...]
    d = x.shape[-1]
    shift = mod_ref[:, 3 * d:4 * d]
    scale = mod_ref[:, 4 * d:5 * d]
    h = (_rms(x) * gpre_ref[...] * (1.0 + scale) + shift).astype(BF16)
    p = _dot(h, win_ref[...])
    nq = GQA_HEADS * GQA_HD
    nk = GQA_KV_HEADS * LANE
    o_qs = nq
    o_k = 2 * nq
    o_ks = o_k + nk
    o_f = o_ks + nk
    qa, qb, ka, kb = tab_ref[0], tab_ref[1], tab_ref[2], tab_ref[3]
    for c in range(nq // MXU_DIM):
        lo = c * MXU_DIM
        qc = p[:, lo:lo + MXU_DIM]
        qs = p[:, o_qs + lo:o_qs + lo + MXU_DIM]
        r = lax.rsqrt(_head_mean_sq(qc, gmat_ref) + EPS)
        for s in range(MXU_DIM // LANE):
            a, b = s * LANE, (s + 1) * LANE
            q_ref[:, lo + a:lo + b] = (r[:, a:b] * (qc[:, a:b] * qa + qs[:, a:b] * qb)).astype(BF16)
    kc = p[:, o_k:o_k + nk]
    ks = p[:, o_ks:o_ks + nk]
    r = lax.rsqrt(_head_mean_sq(kc, gmat_ref) + EPS)
    for s in range(nk // LANE):
        a, b = s * LANE, (s + 1) * LANE
        k_ref[:, a:b] = (r[:, a:b] * (kc[:, a:b] * ka + ks[:, a:b] * kb)).astype(BF16)
    vt_ref[...] = _dot_t(wvt_ref[...], h).astype(BF16)
    nf = FNET_GROUPS * FNET_CH
    for c in range(nf // MXU_DIM):
        lo = c * MXU_DIM
        fx = p[:, o_f + lo:o_f + lo + MXU_DIM].astype(BF16)
        cs = _dot(fx, dft_ref[...])
        f_ref[:, lo:lo + MXU_DIM] = cs[:, :MXU_DIM].astype(BF16)
        f_ref[:, nf + lo:nf + lo + MXU_DIM] = cs[:, MXU_DIM:].astype(BF16)


def _odd_in(x, mod, g_pre, win, wvt, gmat, dft, tabs, rows_per_batch, fixed_row=None):
    r, d = x.shape
    nd = mod.shape[-1]
    tm = _row_tile(rows_per_batch, r)
    per = rows_per_batch // tm
    row = lambda t: (t, 0)
    nq = GQA_HEADS * GQA_HD
    nk = GQA_KV_HEADS * LANE
    nf = FNET_GROUPS * FNET_CH
    if fixed_row is None:
        tab_spec = pl.BlockSpec((4, tm, LANE), lambda t: (0, t % per, 0))
    else:
        tab_spec = pl.BlockSpec((4, tm, LANE), lambda t: (0, 0, 0))
    return pl.pallas_call(
        _odd_in_kernel,
        grid=(r // tm,),
        in_specs=[pl.BlockSpec((tm, d), row),
                  _mod_spec(nd, rows_per_batch, tm, fixed_row),
                  pl.BlockSpec((1, d), lambda t: (0, 0)),
                  _resident(win.shape),
                  _resident(wvt.shape),
                  _resident(gmat.shape),
                  _resident(dft.shape),
                  tab_spec],
        out_specs=[pl.BlockSpec((tm, nq), row), pl.BlockSpec((tm, nk), row),
                   pl.BlockSpec((wvt.shape[0], tm), lambda t: (0, t)),
                   pl.BlockSpec((tm, 2 * nf), row)],
        out_shape=[jax.ShapeDtypeStruct((r, nq), BF16), jax.ShapeDtypeStruct((r, nk), BF16),
                   jax.ShapeDtypeStruct((wvt.shape[0], r), BF16),
                   jax.ShapeDtypeStruct((r, 2 * nf), BF16)],
        compiler_params=_cparams(1),
        name="odd_in",
    )(x, mod, g_pre.reshape(1, d), win, wvt, gmat, dft, tabs)


def _attn_kernel(*refs, n_src, q_slab, q_half, k_slab, v_row):
    q_ref = refs[0]
    k_refs = refs[1:1 + n_src]
    vt_refs = refs[1 + n_src:1 + 2 * n_src]
    o_ref = refs[1 + 2 * n_src]
    s_ref = refs[2 + 2 * n_src]
    tq = q_ref.shape[0]
    lane = lax.broadcasted_iota(jnp.int32, (tq, LANE), 1)
    starts = [0]
    for kr in k_refs:
        starts.append(starts[-1] + kr.shape[0])
    slot = lambda hd: hd % 2

    def scores(hd):
        q = q_ref[:, q_slab[hd] * LANE:(q_slab[hd] + 1) * LANE]
        if q_half[hd] is not None:
            keep = (lane < HEAD_V) if q_half[hd] == 0 else (lane >= HEAD_V)
            q = jnp.where(keep, q, jnp.zeros_like(q))
        ks = k_slab[hd]
        m = None
        for kr, r0 in zip(k_refs, starts):
            si = _dot_t(kr[:, ks * LANE:(ks + 1) * LANE], q)
            s_ref[slot(hd), r0:r0 + kr.shape[0], :] = si
            mi = si.max(axis=0, keepdims=True)
            m = mi if m is None else jnp.maximum(m, mi)
        return m

    m_next = scores(0)
    for hd in range(HEADS):
        m = m_next
        if hd + 1 < HEADS:
            m_next = scores(hd + 1)
        v0 = v_row[hd] * HEAD_V
        den = None
        o = None
        for vr, r0 in zip(vt_refs, starts):
            for c0 in range(0, vr.shape[1], ATTN_KEYS):
                c1 = min(c0 + ATTN_KEYS, vr.shape[1])
                pc = jnp.exp2(s_ref[slot(hd), r0 + c0:r0 + c1, :] - m)
                dc = pc.sum(axis=0, keepdims=True)
                oc = _dot(vr[v0:v0 + HEAD_V, c0:c1], pc.astype(BF16))
                den = dc if den is None else den + dc
                o = oc if o is None else o + oc
        o_ref[hd * HEAD_V:(hd + 1) * HEAD_V, :] = (o / den).astype(BF16)


def _attention(q, kvs, n_batch, lq, head_maps, tq=512):
    tq = min(tq, lq)
    per = lq // tq
    n_src = len(kvs)
    wq = q.shape[1]
    in_specs = [pl.BlockSpec((tq, wq), lambda b, i: (b * per + i, 0))]
    for k, _, lk in kvs:
        in_specs.append(pl.BlockSpec((lk, k.shape[1]), lambda b, i: (b, 0)))
    for _, vt, lk in kvs:
        in_specs.append(pl.BlockSpec((vt.shape[0], lk), lambda b, i: (0, b)))
    kern = functools.partial(_attn_kernel, n_src=n_src, **head_maps)
    return pl.pallas_call(
        kern,
        grid=(n_batch, per),
        in_specs=in_specs,
        out_specs=pl.BlockSpec((HEADS * HEAD_V, tq), lambda b, i: (0, b * per + i)),
        out_shape=jax.ShapeDtypeStruct((HEADS * HEAD_V, n_batch * lq), BF16),
        scratch_shapes=[pltpu.VMEM((2, sum(lk for _, _, lk in kvs), tq), F32)],
        compiler_params=_cparams(2),
        name="attention",
    )(q, *[k for k, _, _ in kvs], *[vt for _, vt, _ in kvs])


_MLA_MAPS = dict(q_slab=tuple(range(8)), q_half=(None,) * 8,
                 k_slab=tuple(range(8)), v_row=tuple(range(8)))
_GQA_MAPS = dict(q_slab=tuple(h // 2 for h in range(8)), q_half=tuple(h % 2 for h in range(8)),
                 k_slab=tuple(h // GQA_GROUP for h in range(8)),
                 v_row=tuple(h // GQA_GROUP for h in range(8)))


def _conv_kernel(u_ref, w_ref, b_ref, g_ref, beta_ref, o_ref, pad_ref):
    n, ch = u_ref.shape
    zeros = jnp.zeros((CONV_HALO, ch), F32)
    pad_ref[0:CONV_HALO, :] = zeros
    pad_ref[CONV_HALO + n:2 * CONV_HALO + n, :] = zeros
    pad_ref[CONV_HALO:CONV_HALO + n, :] = u_ref[...]
    first = CONV_HALO - CONV_K // 2
    span = CONV_ROWS + 2 * CONV_HALO

    def step(i, carry):
        r0 = pl.multiple_of(i * CONV_ROWS, CONV_ROWS)
        cols = []
        for c0 in range(0, ch, LANE):
            win = pad_ref[pl.ds(r0, span), c0:c0 + LANE]
            acc = None
            for s in range(SUBLANE):
                sh = win if s == 0 else pltpu.roll(win, span - s, axis=0)
                sh = sh.reshape(span // SUBLANE, SUBLANE, LANE)
                for k in range(CONV_K):
                    a, sk = divmod(first + k, SUBLANE)
                    if sk == s:
                        term = sh[a:a + CONV_ROWS // SUBLANE] * w_ref[k, :, c0:c0 + LANE][None]
                        acc = term if acc is None else acc + term
            cols.append(acc.reshape(CONV_ROWS, LANE) + b_ref[:, c0:c0 + LANE])
        v = jnp.concatenate(cols, axis=1)
        mu = jnp.mean(v, axis=-1, keepdims=True)
        cen = v - mu
        var = jnp.mean(cen * cen, axis=-1, keepdims=True)
        y = cen * lax.rsqrt(var + EPS) * g_ref[...] + beta_ref[...]
        o_ref[pl.ds(r0, CONV_ROWS), :] = (y * _sigmoid(y)).astype(BF16)
        return carry

    lax.fori_loop(0, n // CONV_ROWS, step, 0)


def _conformer_conv(u, n_batch, n, conv_w, conv_b, ln_g, ln_b):
    ch = u.shape[-1]
    vec = pl.BlockSpec((1, ch), lambda b: (0, 0))
    return pl.pallas_call(
        _conv_kernel,
        grid=(n_batch,),
        in_specs=[pl.BlockSpec((n, ch), lambda b: (b, 0)),
                  pl.BlockSpec((CONV_K, SUBLANE, ch), lambda b: (0, 0, 0)), vec, vec, vec],
        out_specs=pl.BlockSpec((n, ch), lambda b: (b, 0)),
        out_shape=jax.ShapeDtypeStruct((n_batch * n, ch), BF16),
        scratch_shapes=[pltpu.VMEM((n + 2 * CONV_HALO, ch), F32)],
        compiler_params=_cparams(1),
        name="conformer_conv",
    )(u, jnp.broadcast_to(conv_w[:, None, :], (CONV_K, SUBLANE, ch)),
      conv_b.reshape(1, ch), ln_g.reshape(1, ch), ln_b.reshape(1, ch))


def _seq_dft_kernel(x_ref, c_ref, s_ref, o_ref, *, norm):
    tr = o_ref.shape[0]
    nf = o_ref.shape[1]
    r0 = pl.multiple_of(pl.program_id(1) * tr, tr)
    acc = _dot(c_ref[pl.ds(r0, tr), :], x_ref[:, :nf]) - _dot(s_ref[pl.ds(r0, tr), :], x_ref[:, nf:])
    o_ref[...] = (acc * norm).astype(BF16)


def _seq_dft(xf, n_batch, n, cmat, smat):
    nf = xf.shape[1] // 2
    tr = min(512, n)
    per = n // tr
    norm = 1.0 / math.sqrt(n * FNET_CH)
    return pl.pallas_call(
        functools.partial(_seq_dft_kernel, norm=norm),
        grid=(n_batch, per),
        in_specs=[pl.BlockSpec((n, 2 * nf), lambda b, j: (b, 0)),
                  _resident(cmat.shape), _resident(smat.shape)],
        out_specs=pl.BlockSpec((tr, nf), lambda b, j: (b * per + j, 0)),
        out_shape=jax.ShapeDtypeStruct((n_batch * n, nf), BF16),
        compiler_params=_cparams(2),
        name="seq_dft",
    )(xf, cmat, smat)


def _rope_cos_sin(n, rot_dim):
    nf = rot_dim // 4
    t = jnp.arange(n)
    row = (t // GRID_W).astype(F32)
    col = (t % GRID_W).astype(F32)
    inv = ROPE_THETA ** (-jnp.arange(nf, dtype=F32) / nf)
    ang = jnp.concatenate([row[:, None] * inv, col[:, None] * inv], axis=-1)
    cos, sin = jnp.cos(ang), jnp.sin(ang)
    lanes = np.arange(rot_dim)
    half = (lanes % (2 * nf)) // nf
    src = (lanes // (2 * nf)) * nf + lanes % nf
    sign = np.where(half == 0, -1.0, 1.0).astype(np.float32)
    partner = np.where(half == 0, lanes + nf, lanes - nf)
    return cos[:, src], sin[:, src] * sign, partner


def _mla_tables(n, tm_ctx):
    cos, sin, partner = _rope_cos_sin(n, MLA_ROPE)
    ones = jnp.ones((n, MLA_NOPE), F32)
    zer = jnp.zeros((n, MLA_NOPE), F32)
    pad = jnp.zeros((n, LANE - MLA_NOPE - MLA_ROPE), F32)
    qa = jnp.concatenate([ones, cos, pad], axis=1) * MLA_SCALE
    qb = jnp.concatenate([zer, sin, pad], axis=1) * MLA_SCALE
    ka = jnp.concatenate([zer, cos, pad], axis=1)
    kb = jnp.concatenate([zer, sin, pad], axis=1)
    lane = np.arange(LANE)
    c_qa = np.where(lane < MLA_NOPE + MLA_ROPE, MLA_SCALE, 0.0).astype(np.float32)
    c_ka = np.where((lane >= MLA_NOPE) & (lane < MLA_NOPE + MLA_ROPE), 1.0, 0.0).astype(np.float32)
    ctx = [jnp.broadcast_to(jnp.asarray(c), (tm_ctx, LANE))
           for c in (c_qa, np.zeros(LANE, np.float32), c_ka, np.zeros(LANE, np.float32))]
    lat = jnp.stack([qa, qb, ka, kb])
    return lat, jnp.stack(ctx), partner


def _gqa_tables(n, tm_ctx, g_qn, g_kn):
    cos, sin, partner = _rope_cos_sin(n, GQA_HD)
    two = lambda a: jnp.concatenate([a, a], axis=-1)
    qa = two(cos * g_qn) * GQA_SCALE
    qb = two(sin * g_qn[partner]) * GQA_SCALE
    ka = two(cos * g_kn)
    kb = two(sin * g_kn[partner])
    lat = jnp.stack([qa, qb, ka, kb])
    zero = jnp.zeros((tm_ctx, LANE), F32)
    ctx = jnp.stack([jnp.broadcast_to(two(g_qn) * GQA_SCALE, (tm_ctx, LANE)), zero,
                     jnp.broadcast_to(two(g_kn), (tm_ctx, LANE)), zero])
    return lat, ctx, partner


def _even_weights(w_in, w_uq, w_ukv, partner):
    d = w_in.shape[0]
    a_cols = 2 * CONV_CH
    kpe0 = a_cols + Q_LORA + KV_LORA
    rope_lo = MLA_NOPE
    zpad = lambda n: jnp.zeros((d, n), F32)
    kpe = w_in[:, kpe0:kpe0 + MLA_ROPE]
    kpe_slab = jnp.concatenate([zpad(rope_lo), kpe, zpad(LANE - rope_lo - MLA_ROPE)], axis=1)
    kpe_sw_slab = jnp.concatenate([zpad(rope_lo), kpe[:, partner],
                                   zpad(LANE - rope_lo - MLA_ROPE)], axis=1)
    win = jnp.concatenate([w_in[:, :kpe0], kpe_slab, kpe_sw_slab], axis=1).astype(BF16)

    dk = MLA_NOPE + MLA_ROPE
    wq3 = w_uq.reshape(Q_LORA, MLA_HEADS, dk)
    qz = jnp.zeros((Q_LORA, MLA_HEADS, LANE - dk), F32)
    wq_pad = jnp.concatenate([wq3, qz], axis=2)
    wq_sw = jnp.concatenate([jnp.zeros((Q_LORA, MLA_HEADS, MLA_NOPE), F32),
                             wq3[:, :, MLA_NOPE:][:, :, partner], qz], axis=2)
    wq = jnp.concatenate([wq_pad.reshape(Q_LORA, -1), wq_sw.reshape(Q_LORA, -1)], axis=1).astype(BF16)

    wkv3 = w_ukv.reshape(KV_LORA, MLA_HEADS, MLA_NOPE + MLA_V)
    kz = jnp.zeros((KV_LORA, MLA_HEADS, LANE - MLA_NOPE), F32)
    wk = jnp.concatenate([wkv3[:, :, :MLA_NOPE], kz], axis=2).reshape(KV_LORA, -1).astype(BF16)
    wvt = wkv3[:, :, MLA_NOPE:].reshape(KV_LORA, -1).T.astype(BF16)
    return win, wq, wk, wvt


def _odd_weights(w_in, partner):
    d = w_in.shape[0]
    nq = GQA_HEADS * GQA_HD
    nkv = GQA_KV_HEADS * GQA_HD
    wq = w_in[:, :nq].reshape(d, GQA_HEADS, GQA_HD)
    wk = w_in[:, nq:nq + nkv].reshape(d, GQA_KV_HEADS, GQA_HD)
    wv = w_in[:, nq + nkv:nq + 2 * nkv].reshape(d, GQA_KV_HEADS, GQA_HD)
    wf = w_in[:, nq + 2 * nkv:]
    dup = lambda a: jnp.concatenate([a, a], axis=2).reshape(d, -1)
    win = jnp.concatenate([wq.reshape(d, -1), wq[:, :, partner].reshape(d, -1),
                           dup(wk), dup(wk[:, :, partner]), wf], axis=1).astype(BF16)
    return win, wv.reshape(d, -1).T.astype(BF16)


def _head_mean_matrix():
    i = np.arange(MXU_DIM)
    same = (i[:, None] // GQA_HD) == (i[None, :] // GQA_HD)
    return jnp.asarray(np.where(same, 1.0 / GQA_HD, 0.0), BF16)


def _channel_dft_matrix():
    i = np.arange(MXU_DIM)
    ang = 2.0 * np.pi * ((i[:, None] % FNET_CH) * (i[None, :] % FNET_CH) % FNET_CH) / FNET_CH
    same = (i[:, None] // FNET_CH) == (i[None, :] // FNET_CH)
    c = np.where(same, np.cos(ang), 0.0)
    s = np.where(same, np.sin(ang), 0.0)
    return jnp.asarray(np.concatenate([c, s], axis=1), BF16)


def _seq_dft_matrices(n):
    i = jnp.arange(n, dtype=jnp.int32)
    ang = ((i[:, None] * i[None, :]) % n).astype(F32) * (2.0 * math.pi / n)
    return jnp.cos(ang).astype(BF16), jnp.sin(ang).astype(BF16)


def kernel(x, c, ctx, c_ctx, w_mod, b_mod, g_pre, g_post, ffn_w_gate, ffn_w_up, ffn_w_down,
           ev_w_in, ev_conv_w, ev_conv_b, ev_ln_g, ev_ln_b, ev_g_cq, ev_w_uq, ev_g_ckv,
           ev_w_ukv, ev_w_out, od_w_in, od_g_qn, od_g_kn, od_w_out):
    n_b, n_l, d = x.shape
    n_c = ctx.shape[1]
    depth = w_mod.shape[0]
    nd = w_mod.shape[2]

    bp = -(-(n_b + 1) // SUBLANE) * SUBLANE
    c_all = jnp.concatenate([c, c_ctx[None, :], jnp.zeros((bp - n_b - 1, d), F32)], axis=0)
    mod_all = _modulation(c_all, w_mod, b_mod).reshape(depth, bp, 1, nd)

    xl = x.reshape(n_b * n_l, d)
    xc = ctx.reshape(n_b * n_c, d)
    ctx_rows = n_b * n_c
    tm_c = _row_tile(ctx_rows, ctx_rows)

    mla_lat, mla_ctx, mla_partner = _mla_tables(n_l, tm_c)
    gmat = _head_mean_matrix()
    dft_ch = _channel_dft_matrix()
    cmat_l, smat_l = _seq_dft_matrices(n_l)
    cmat_c, smat_c = _seq_dft_matrices(n_c)

    ffn_w = (ffn_w_gate.astype(BF16), ffn_w_up.astype(BF16), ffn_w_down.astype(BF16))

    for i in range(depth):
        need_ctx = i < depth - 1
        mod = mod_all[i]
        j = i // 2
        xl = _ffn(xl, mod, 0, g_pre[i, 0], g_post[i, 0], ffn_w, (i, 0), rows_per_batch=n_l)
        xc = _ffn(xc, mod, 0, g_pre[i, 0], g_post[i, 0], ffn_w, (i, 0), rows_per_batch=ctx_rows,
                  fixed_row=n_b)
        if i % 2 == 0:
            win, wq, wk, wvt = _even_weights(ev_w_in[j], ev_w_uq[j], ev_w_ukv[j], mla_partner)
            proj = (g_pre[i, 1], win, ev_g_cq[j], wq, ev_g_ckv[j], wk, wvt)
            u_l, q_l, k_l, v_l = _even_in(xl, mod, *proj, mla_lat, rows_per_batch=n_l)
            u_c, q_c, k_c, v_c = _even_in(xc, mod, *proj, mla_ctx, rows_per_batch=ctx_rows,
                                          fixed_row=n_b)
            conv = (ev_conv_w[j], ev_conv_b[j], ev_ln_g[j], ev_ln_b[j])
            ob_l = _attention(q_l, [(k_c, v_c, n_c), (k_l, v_l, n_l)], n_b, n_l, _MLA_MAPS)
            oa_l = _conformer_conv(u_l, n_b, n_l, *conv)
            if need_ctx:
                ob_c = _attention(q_c, [(k_c, v_c, n_c)], n_b, n_c, _MLA_MAPS)
                oa_c = _conformer_conv(u_c, n_b, n_c, *conv)
            w_out = ev_w_out[j].astype(BF16)
        else:
            gqa_lat, gqa_ctx, gqa_partner = _gqa_tables(n_l, tm_c, od_g_qn[j], od_g_kn[j])
            win, wvt = _odd_weights(od_w_in[j], gqa_partner)
            proj = (g_pre[i, 1], win, wvt, gmat, dft_ch)
            q_l, k_l, v_l, f_l = _odd_in(xl, mod, *proj, gqa_lat, rows_per_batch=n_l)
            q_c, k_c, v_c, f_c = _odd_in(xc, mod, *proj, gqa_ctx, rows_per_batch=ctx_rows,
                                         fixed_row=n_b)
            oa_l = _attention(q_l, [(k_c, v_c, n_c), (k_l, v_l, n_l)], n_b, n_l, _GQA_MAPS)
            ob_l = _seq_dft(f_l, n_b, n_l, cmat_l, smat_l)
            if need_ctx:
                oa_c = _attention(q_c, [(k_c, v_c, n_c)], n_b, n_c, _GQA_MAPS)
                ob_c = _seq_dft(f_c, n_b, n_c, cmat_c, smat_c)
            w_out = od_w_out[j].astype(BF16)
        xl = _ffn(xl, mod, 2, g_pre[i, 2], g_post[i, 2], ffn_w, (i, 1), rows_per_batch=n_l,
                  mix=(oa_l, ob_l, w_out, g_post[i, 1], i % 2 == 1))
        if need_ctx:
            xc = _ffn(xc, mod, 2, g_pre[i, 2], g_post[i, 2], ffn_w, (i, 1), rows_per_batch=ctx_rows,
                      fixed_row=n_b, mix=(oa_c, ob_c, w_out, g_post[i, 1], i % 2 == 1))
    return xl.reshape(n_b, n_l, d)
```
